```python
import math
import jax, jax.numpy as jnp
from jax import lax
import numpy as np

D_MODEL = 1024
BATCH = 2
SEQ = 8192
DEPTH = 4

HEAD_DIM = 64
N_HEADS_PER_MIXER = 4
GROUP_WIDTH = N_HEADS_PER_MIXER * HEAD_DIM
N_MIXERS = 4
MIX_WIDTH = N_MIXERS * GROUP_WIDTH
DIFF_QK_DIM = HEAD_DIM // 2
Q_BLOCK = 128
SGU_CHUNK = 128
DILATED_PATTERNS = ((128, 1), (512, 4), (2048, 16))
DIL_BLOCK = 128
MLSTM_CHUNK = 128
CONV_WIDTH = 4
ROPE_THETA = 500000.0
ROT_FRACTION = 4
D_FF = -(-8 * D_MODEL // (3 * 256)) * 256
ALPHA = (2 * DEPTH) ** 0.25
BETA = (8 * DEPTH) ** -0.25
LN_EPS = 1e-5
SPLIT_SIZES = (GROUP_WIDTH,) * 3 + (GROUP_WIDTH,) * 2 + (GROUP_WIDTH,) * 3 + (GROUP_WIDTH,) * 4 + (N_HEADS_PER_MIXER, N_HEADS_PER_MIXER)
IN_WIDTH = sum(SPLIT_SIZES)
SPLIT_IDX = tuple(int(v) for v in np.cumsum(SPLIT_SIZES)[:-1])

kernel_name = "hymba_style_diff_sgu_dilated_mlstm"


def layer_norm(x, g, b):
    xf = x.astype(jnp.float32)
    mu = xf.mean(-1, keepdims=True)
    var = jnp.square(xf - mu).mean(-1, keepdims=True)
    return ((xf - mu) * lax.rsqrt(var + LN_EPS) * g + b).astype(x.dtype)


def rms_norm(x, g):
    xf = x.astype(jnp.float32)
    return (xf * lax.rsqrt(jnp.square(xf).mean(-1, keepdims=True) + LN_EPS) * g).astype(x.dtype)


def rope_tables(seq, head_dim):
    rot = head_dim // ROT_FRACTION
    pos = jnp.arange(seq, dtype=jnp.float32)
    inv = ROPE_THETA ** (-jnp.arange(0, rot, 2, dtype=jnp.float32) / rot)
    ang = pos[:, None] * inv[None, :]
    return jnp.cos(ang), jnp.sin(ang)


def partial_rope(x, cos, sin):
    half = cos.shape[-1]
    x1, x2, xp = x[..., :half], x[..., half:2 * half], x[..., 2 * half:]
    c = cos[None, :, None, :].astype(x.dtype)
    s = sin[None, :, None, :].astype(x.dtype)
    return jnp.concatenate([x1 * c - x2 * s, x2 * c + x1 * s, xp], axis=-1)


def diff_attention(q, k, v, lam):
    B, S, H, _, Dk = q.shape
    nq = S // Q_BLOCK
    scale = Dk ** -0.5
    qb = jnp.moveaxis(q.reshape(B, nq, Q_BLOCK, H, 2, Dk), 1, 0)
    kpos = jnp.arange(S)

    def block(args):
        qi, idx = args
        s = jnp.einsum('bqhmd,bkhmd->bhmqk', qi, k).astype(jnp.float32) * scale
        qpos = idx * Q_BLOCK + jnp.arange(Q_BLOCK)
        s = jnp.where((kpos[None, :] <= qpos[:, None])[None, None, None], s, -jnp.inf)
        p = jax.nn.softmax(s, axis=-1)
        a = p[:, :, 0] - lam * p[:, :, 1]
        return jnp.einsum('bhqk,bkhd->bqhd', a.astype(v.dtype), v)

    o = lax.map(block, (qb, jnp.arange(nq)))
    return jnp.moveaxis(o, 0, 1).reshape(B, S, H, v.shape[-1])


def spatial_gating(u, v, ln_g, ln_b, w_s, b_s):
    B, S, W = v.shape
    G, C, _ = w_s.shape
    nc = S // C
    u = jax.nn.gelu(u)
    v = layer_norm(jax.nn.gelu(v), ln_g, ln_b)
    vc = v.reshape(B, nc, C, G, W // G)
    w = w_s * jnp.tril(jnp.ones((C, C), w_s.dtype))[None]
    z = jnp.einsum('gts,bnsgd->bntgd', w, vc) + jnp.transpose(b_s)[None, None, :, :, None]
    return u * z.reshape(B, S, W)


def dilated_pattern(q, k, v, window, dilation):
    B, S, H, D = q.shape
    I = DIL_BLOCK
    span = dilation * I
    s_pad = -(-S // span) * span
    nb = s_pad // span

    def to_blocks(t):
        t = jnp.pad(t, ((0, 0), (0, s_pad - S), (0, 0), (0, 0)))
        return t.reshape(B, nb, I, dilation, H, D)

    def with_prev(t):
        prev = jnp.pad(t[:, :-1], ((0, 0), (1, 0), (0, 0), (0, 0), (0, 0), (0, 0)))
        return jnp.concatenate([prev, t], axis=2)

    qb = to_blocks(q)
    kk, vv = with_prev(to_blocks(k)), with_prev(to_blocks(v))
    s = jnp.einsum('bnirhd,bnjrhd->bnrhij', qb, kk).astype(jnp.float32)
    i_idx = jnp.arange(I)[:, None]
    j_idx = jnp.arange(2 * I)[None, :]
    dist = I + i_idx - j_idx
    band = (dist >= 0) & (dist <= window // dilation)
    start_ok = (jnp.arange(nb)[:, None, None] * I + j_idx[None] - I) >= 0
    valid = band[None] & start_ok
    s = jnp.where(valid[None, :, None, None], s, -jnp.inf)
    m = s.max(-1, keepdims=True)
    p = jnp.exp(s - m)
    den = p.sum(-1, keepdims=True)
    o = jnp.einsum('bnrhij,bnjrhd->bnirhd', (p / den).astype(v.dtype), vv)
    lse = (m + jnp.log(den))[..., 0]
    o = o.reshape(B, s_pad, H, D)[:, :S]
    lse = jnp.transpose(lse, (0, 1, 4, 2, 3)).reshape(B, s_pad, H)[:, :S]
    return o, lse


def dilated_mixture(q, k, v):
    outs, lses = [], []
    for window, dilation in DILATED_PATTERNS:
        o, l = dilated_pattern(q, k, v, window, dilation)
        outs.append(o)
        lses.append(l)
    wts = jax.nn.softmax(jnp.stack(lses, 0), axis=0)
    return jnp.einsum('pbsh,pbshd->bshd', wts.astype(q.dtype), jnp.stack(outs, 0))


def causal_depthwise_conv(x, w, b):
    K, C = w.shape
    y = lax.conv_general_dilated(x, w[:, None, :], window_strides=(1,), padding=[(K - 1, 0)],
                                 dimension_numbers=('NWC', 'WIO', 'NWC'), feature_group_count=C)
    return y + b


def mlstm_chunkwise(q, k, v, i_pre, f_pre):
    B, S, H, D = q.shape
    L = MLSTM_CHUNK
    nc = S // L
    f32 = jnp.float32
    qc = q.astype(f32).reshape(B, nc, L, H, D)
    kc = (k.astype(f32) * D ** -0.5).reshape(B, nc, L, H, D)
    vc = v.astype(f32).reshape(B, nc, L, H, D)
    ic = i_pre.astype(f32).reshape(B, nc, L, H)
    b = jnp.cumsum(jax.nn.log_sigmoid(f_pre.astype(f32)).reshape(B, nc, L, H), axis=2)
    b_last = b[:, :, -1]
    causal = jnp.tril(jnp.ones((L, L), bool))
    d_log = jnp.where(causal[None, None, :, :, None],
                      b[:, :, :, None] - b[:, :, None] + ic[:, :, None], -jnp.inf)
    g = b_last[:, :, None] - b + ic
    g_max = g.max(axis=2)
    w = jnp.exp(g - g_max[:, :, None])
    c_loc = jnp.einsum('bnsh,bnshd,bnshe->bnhde', w, vc, kc)
    n_loc = jnp.einsum('bnsh,bnshe->bnhe', w, kc)

    def step(carry, xs):
        c, n, m = carry
        cl, nl, gm, bl = xs
        m_new = jnp.maximum(bl + m, gm)
        a = jnp.exp(bl + m - m_new)
        e = jnp.exp(gm - m_new)
        return (a[..., None, None] * c + e[..., None, None] * cl,
                a[..., None] * n + e[..., None] * nl, m_new), (c, n, m)

    init = (jnp.zeros((B, H, D, D), f32), jnp.zeros((B, H, D), f32), jnp.zeros((B, H), f32))
    sw_ax = lambda t: jnp.moveaxis(t, 1, 0)
    _, (c_prev, n_prev, m_prev) = lax.scan(step, init, (sw_ax(c_loc), sw_ax(n_loc), sw_ax(g_max), sw_ax(b_last)))
    c_prev, n_prev, m_prev = sw_ax(c_prev), sw_ax(n_prev), sw_ax(m_prev)
    inter_log = b + m_prev[:, :, None]
    m_out = jnp.maximum(inter_log, d_log.max(axis=3))
    p = jnp.exp(d_log - m_out[:, :, :, None])
    sw = p * jnp.einsum('bnthd,bnshd->bntsh', qc, kc)
    e = jnp.exp(inter_log - m_out)
    num = jnp.einsum('bntsh,bnshd->bnthd', sw, vc) + e[..., None] * jnp.einsum('bnhde,bnthe->bnthd', c_prev, qc)
    den = sw.sum(axis=3) + e * jnp.einsum('bnhe,bnthe->bnth', n_prev, qc)
    h = num / jnp.maximum(jnp.abs(den), jnp.exp(-m_out))[..., None]
    return h.reshape(B, S, H, D)


def mixer_block(x, w_in, lam_vecs, lam_init, subln_g, sgu_ln_g, sgu_ln_b, sgu_w, sgu_b,
                conv_w, conv_b, gate_b, mnorm_g, w_out, rope_a, rope_c):
    B, S, _ = x.shape
    H = N_HEADS_PER_MIXER
    z = jnp.einsum('bsd,df->bsf', x, w_in)
    (a_q, a_k, a_v, b_u, b_v, c_q, c_k, c_v,
     d_q, d_k, d_v, d_o, d_i, d_f) = jnp.split(z, SPLIT_IDX, axis=-1)

    aq = partial_rope(a_q.reshape(B, S, 2 * H, DIFF_QK_DIM), *rope_a).reshape(B, S, H, 2, DIFF_QK_DIM)
    ak = partial_rope(a_k.reshape(B, S, 2 * H, DIFF_QK_DIM), *rope_a).reshape(B, S, H, 2, DIFF_QK_DIM)
    lv = lam_vecs.astype(jnp.float32)
    lam = jnp.exp(jnp.sum(lv[0] * lv[1])) - jnp.exp(jnp.sum(lv[2] * lv[3])) + lam_init
    out_a = diff_attention(aq, ak, a_v.reshape(B, S, H, HEAD_DIM), lam)
    out_a = rms_norm(out_a, subln_g) * (1.0 - lam_init)

    out_b = spatial_gating(b_u, b_v, sgu_ln_g, sgu_ln_b, sgu_w, sgu_b)

    cq = partial_rope(c_q.reshape(B, S, H, HEAD_DIM), *rope_c) * HEAD_DIM ** -0.5
    ck = partial_rope(c_k.reshape(B, S, H, HEAD_DIM), *rope_c)
    out_c = dilated_mixture(cq, ck, c_v.reshape(B, S, H, HEAD_DIM))

    qk = jax.nn.silu(causal_depthwise_conv(jnp.concatenate([d_q, d_k], -1), conv_w, conv_b))
    mq, mk = qk[..., :GROUP_WIDTH], qk[..., GROUP_WIDTH:]
    h = mlstm_chunkwise(mq.reshape(B, S, H, HEAD_DIM), mk.reshape(B, S, H, HEAD_DIM),
                        d_v.reshape(B, S, H, HEAD_DIM), d_i + gate_b[0], d_f + gate_b[1])
    h = layer_norm(h, mnorm_g, 0.0).astype(x.dtype)
    out_d = jax.nn.sigmoid(d_o.reshape(B, S, H, HEAD_DIM)) * h

    mixed = jnp.concatenate([out_a.reshape(B, S, GROUP_WIDTH).astype(x.dtype), out_b.astype(x.dtype),
                             out_c.reshape(B, S, GROUP_WIDTH).astype(x.dtype),
                             out_d.reshape(B, S, GROUP_WIDTH).astype(x.dtype)], axis=-1)
    return jnp.einsum('bsf,fd->bsd', mixed, w_out)


def swiglu(x, w_gate, w_up, w_down):
    g = jnp.einsum('bsd,df->bsf', x, w_gate)
    u = jnp.einsum('bsd,df->bsf', x, w_up)
    return jnp.einsum('bsf,fd->bsd', jax.nn.silu(g) * u, w_down)


def setup_inputs(seed: int = 0) -> dict:
    key = jax.random.key(seed)
    ks = jax.random.split(key, 24)
    L, H = DEPTH, N_HEADS_PER_MIXER
    nrm = lambda k, shape, scale: jax.random.normal(k, shape, jnp.float32) * scale
    return {
        "x": nrm(ks[0], (BATCH, SEQ, D_MODEL), 1.0),
        "w_in": nrm(ks[1], (L, D_MODEL, IN_WIDTH), D_MODEL ** -0.5),
        "diff_lambda": nrm(ks[2], (L, 4, DIFF_QK_DIM), 0.1),
        "diff_subln_g": 1.0 + nrm(ks[3], (L, HEAD_DIM), 0.02),
        "sgu_ln_g": 1.0 + nrm(ks[4], (L, GROUP_WIDTH), 0.02),
        "sgu_ln_b": nrm(ks[5], (L, GROUP_WIDTH), 0.02),
        "sgu_w": nrm(ks[6], (L, H, SGU_CHUNK, SGU_CHUNK), SGU_CHUNK ** -0.5),
        "sgu_b": 1.0 + nrm(ks[7], (L, H, SGU_CHUNK), 0.02),
        "mlstm_conv_w": nrm(ks[8], (L, CONV_WIDTH, 2 * GROUP_WIDTH), CONV_WIDTH ** -0.5),
        "mlstm_conv_b": nrm(ks[9], (L, 2 * GROUP_WIDTH), 0.02),
        "mlstm_gate_b": nrm(ks[10], (L, 2, H), 0.1) + jnp.array([0.0, 3.0], jnp.float32)[None, :, None],
        "mlstm_norm_g": 1.0 + nrm(ks[11], (L, HEAD_DIM), 0.02),
        "w_out": nrm(ks[12], (L, MIX_WIDTH, D_MODEL), MIX_WIDTH ** -0.5 * BETA),
        "ln1_g": 1.0 + nrm(ks[13], (L, D_MODEL), 0.02),
        "ln1_b": nrm(ks[14], (L, D_MODEL), 0.02),
        "w_gate": nrm(ks[15], (L, D_MODEL, D_FF), D_MODEL ** -0.5),
        "w_up": nrm(ks[16], (L, D_MODEL, D_FF), D_MODEL ** -0.5),
        "w_down": nrm(ks[17], (L, D_FF, D_MODEL), D_FF ** -0.5 * BETA),
        "ln2_g": 1.0 + nrm(ks[18], (L, D_MODEL), 0.02),
        "ln2_b": nrm(ks[19], (L, D_MODEL), 0.02),
    }


def reference(x, w_in, diff_lambda, diff_subln_g, sgu_ln_g, sgu_ln_b, sgu_w, sgu_b,
              mlstm_conv_w, mlstm_conv_b, mlstm_gate_b, mlstm_norm_g, w_out,
              ln1_g, ln1_b, w_gate, w_up, w_down, ln2_g, ln2_b):
    S = x.shape[1]
    rope_a = rope_tables(S, DIFF_QK_DIM)
    rope_c = rope_tables(S, HEAD_DIM)
    for l in range(DEPTH):
        lam_init = 0.8 - 0.6 * math.exp(-0.3 * l)
        h = mixer_block(x, w_in[l], diff_lambda[l], lam_init, diff_subln_g[l], sgu_ln_g[l], sgu_ln_b[l],
                        sgu_w[l], sgu_b[l], mlstm_conv_w[l], mlstm_conv_b[l], mlstm_gate_b[l],
                        mlstm_norm_g[l], w_out[l], rope_a, rope_c)
        x = layer_norm(ALPHA * x + h, ln1_g[l], ln1_b[l])
        f = swiglu(x, w_gate[l], w_up[l], w_down[l])
        x = layer_norm(ALPHA * x + f, ln2_g[l], ln2_b[l])
    return x
```

```python
import functools
import math

import jax
import jax.numpy as jnp
from jax import lax
from jax.experimental import pallas as pl
from jax.experimental.pallas import tpu as pltpu

F32 = jnp.float32
BF16 = jnp.bfloat16

D_MODEL = 1024
DEPTH = 4
N_HEADS = 4
HEAD_DIM = 64
GROUP = N_HEADS * HEAD_DIM
DIFF_QK = HEAD_DIM // 2
CHUNK = 128
DILATIONS = (1, 4, 16)
SPAN = CHUNK * DILATIONS[-1]
CONV_WIDTH = 4
ROPE_THETA = 500000.0
D_FF = 2816
IN_WIDTH = 3080
IN_PAD = 3200
GATE_BLOCK = 3072 // 128
ALPHA = (2 * DEPTH) ** 0.25
LN_EPS = 1e-5
NEG_INF = float("-inf")
VMEM_LIMIT = 56 * 1024 * 1024


def _cparams(sem):
    return pltpu.CompilerParams(dimension_semantics=sem, vmem_limit_bytes=VMEM_LIMIT)


def _head_select(head_of_lane, parts):
    out = parts[-1]
    for h in range(len(parts) - 2, -1, -1):
        out = jnp.where(head_of_lane == h, parts[h], out)
    return out


def _layer_norm_rows(y, g, b):
    mu = jnp.mean(y, axis=-1, keepdims=True)
    yc = y - mu
    var = jnp.mean(yc * yc, axis=-1, keepdims=True)
    return yc * lax.rsqrt(var + LN_EPS) * g + b


def _inproj_kernel(x_ref, w_ref, o_ref):
    o_ref[...] = jnp.dot(x_ref[...].astype(BF16), w_ref[...], preferred_element_type=F32)


def _inproj(x2, w):
    m = x2.shape[0]
    tm, tn = 512, 640
    return pl.pallas_call(
        _inproj_kernel,
        grid=(m // tm, IN_PAD // tn),
        in_specs=[pl.BlockSpec((tm, D_MODEL), lambda i, j: (i, 0)),
                  pl.BlockSpec((D_MODEL, tn), lambda i, j: (0, j))],
        out_specs=pl.BlockSpec((tm, tn), lambda i, j: (i, j)),
        out_shape=jax.ShapeDtypeStruct((m, IN_PAD), F32),
        compiler_params=_cparams(("parallel", "arbitrary")),
        name="inproj",
    )(x2, w)


def _rope(x, c, s_lo, s_hi, half):
    width = x.shape[1]
    return x * c + pltpu.roll(x, width - half, 1) * s_lo + pltpu.roll(x, half, 1) * s_hi


def _prep_kernel(aq_ref, ak_ref, av_ref, cq_ref, ck_ref, ta_ref, tc_ref,
                 oaq_ref, oak_ref, oav_ref, ocq_ref, ock_ref):
    ta_c, ta_lo, ta_hi = ta_ref[0], ta_ref[1], ta_ref[2]
    tc_c, tc_lo, tc_hi = tc_ref[0], tc_ref[1], tc_ref[2]
    half_a = DIFF_QK // 8
    half_c = HEAD_DIM // 8
    oaq_ref[...] = (_rope(aq_ref[...], ta_c, ta_lo, ta_hi, half_a) * (DIFF_QK ** -0.5)).astype(BF16)
    oak_ref[...] = _rope(ak_ref[...], ta_c, ta_lo, ta_hi, half_a).astype(BF16)
    oav_ref[...] = av_ref[...].astype(BF16)
    ocq_ref[...] = _rope(cq_ref[...], tc_c, tc_lo, tc_hi, half_c) * (HEAD_DIM ** -0.5)
    ock_ref[...] = _rope(ck_ref[...], tc_c, tc_lo, tc_hi, half_c)


def _prep(z, tab_a, tab_c, seq):
    m = z.shape[0]
    tm = 512
    nt = seq // tm
    zspec = lambda c: pl.BlockSpec((tm, GROUP), lambda i, c=c: (i, c))
    tspec = pl.BlockSpec((3, tm, GROUP), lambda i: (0, i % nt, 0))
    ospec = pl.BlockSpec((tm, GROUP), lambda i: (i, 0))
    return pl.pallas_call(
        _prep_kernel,
        grid=(m // tm,),
        in_specs=[zspec(0), zspec(1), zspec(2), zspec(5), zspec(6), tspec, tspec],
        out_specs=[ospec] * 5,
        out_shape=[jax.ShapeDtypeStruct((m, GROUP), BF16)] * 3 + [jax.ShapeDtypeStruct((m, GROUP), F32)] * 2,
        compiler_params=_cparams(("parallel",)),
        name="prep",
    )(z, z, z, z, z, tab_a, tab_c)


def _rope_table(seq, group_dim, n_groups):
    rot = group_dim // 4
    half = rot // 2
    pos = jnp.arange(seq, dtype=F32)
    inv = ROPE_THETA ** (-jnp.arange(0, rot, 2, dtype=F32) / rot)
    ang = pos[:, None] * inv[None, :]
    cos, sin = jnp.cos(ang), jnp.sin(ang)
    ones = jnp.ones((seq, group_dim - rot), F32)
    zeros_half = jnp.zeros((seq, half), F32)
    zeros_rest = jnp.zeros((seq, group_dim - rot), F32)
    c = jnp.concatenate([cos, cos, ones], axis=1)
    lo = jnp.concatenate([-sin, zeros_half, zeros_rest], axis=1)
    hi = jnp.concatenate([zeros_half, sin, zeros_rest], axis=1)
    return jnp.stack([jnp.tile(t, (1, n_groups)) for t in (c, lo, hi)], axis=0)


def _diff_attn_kernel(lamv_ref, g_ref, q_ref, k_ref, v_ref, o_ref, m_ref, l_ref, acc_ref,
                      *, tq, tk, lam_init):
    i = pl.program_id(1)
    lane = lax.broadcasted_iota(jnp.int32, (1, GROUP), 1)
    map_of_lane = lane // DIFF_QK
    head_of_lane = lane // HEAD_DIM
    q = q_ref[...]
    q_zero = jnp.zeros_like(q)
    rep = tk // 128

    m_ref[...] = jnp.full(m_ref.shape, NEG_INF, F32)
    l_ref[...] = jnp.zeros(l_ref.shape, F32)
    acc_ref[...] = jnp.zeros(acc_ref.shape, F32)

    q_pos0 = i * tq
    n_full = q_pos0 // tk

    def step(j, masked):
        ks = pl.multiple_of(j * tk, tk)
        k = k_ref[pl.ds(ks, tk), :]
        v = v_ref[pl.ds(ks, tk), :]
        v_zero = jnp.zeros_like(v)
        v_bd = jnp.concatenate([jnp.where(head_of_lane == h, v, v_zero) for h in range(N_HEADS)], axis=0)
        if masked:
            row = q_pos0 + lax.broadcasted_iota(jnp.int32, (tq, tk), 0)
            col = ks + lax.broadcasted_iota(jnp.int32, (tq, tk), 1)
            keep = col <= row
        probs = [[None] * N_HEADS for _ in range(2)]
        alphas = [[None] * N_HEADS for _ in range(2)]
        for hm in range(2 * N_HEADS):
            qm = jnp.where(map_of_lane == hm, q, q_zero)
            s = lax.dot_general(qm, k, (((1,), (1,)), ((), ())), preferred_element_type=F32)
            if masked:
                s = jnp.where(keep, s, NEG_INF)
            m_prev = m_ref[hm]
            m_new = jnp.maximum(m_prev, jnp.max(s, axis=1, keepdims=True))
            alpha = jnp.exp(m_prev - m_new)
            p = jnp.exp(s - jnp.concatenate([m_new] * rep, axis=1))
            l_ref[hm] = alpha * l_ref[hm] + jnp.sum(p, axis=1, keepdims=True)
            m_ref[hm] = m_new
            probs[hm % 2][hm // 2] = p.astype(BF16)
            alphas[hm % 2][hm // 2] = jnp.concatenate([alpha, alpha], axis=1)
        for mm in range(2):
            p_all = jnp.concatenate(probs[mm], axis=1)
            acc_ref[mm] = (acc_ref[mm] * _head_select(head_of_lane, alphas[mm])
                           + jnp.dot(p_all, v_bd, preferred_element_type=F32))

    def full_step(j, carry):
        step(j, False)
        return carry

    lax.fori_loop(0, n_full, full_step, 0)
    step(n_full, True)

    lv = lamv_ref[...]
    lam = (jnp.exp(jnp.sum(lv[0:1] * lv[1:2], axis=1, keepdims=True))
           - jnp.exp(jnp.sum(lv[2:3] * lv[3:4], axis=1, keepdims=True)) + lam_init)
    outs = []
    for mm in range(2):
        den = [jnp.concatenate([l_ref[2 * h + mm]] * 2, axis=1) for h in range(N_HEADS)]
        outs.append(acc_ref[mm] / _head_select(head_of_lane, den))
    a = outs[0] - lam * outs[1]
    sq = a * a
    ms = [jnp.sum(jnp.where(head_of_lane == h, sq, 0.0), axis=1, keepdims=True) * (1.0 / HEAD_DIM)
          for h in range(N_HEADS)]
    o_ref[...] = a * lax.rsqrt(_head_select(head_of_lane, ms) + LN_EPS) * g_ref[...] * (1.0 - lam_init)


def _diff_attn(aq, ak, av, lam_vecs, subln_g, lam_init, batch, seq):
    tq, tk = 256, 512
    nq = seq // tq
    kern = functools.partial(_diff_attn_kernel, tq=tq, tk=tk, lam_init=lam_init)
    return pl.pallas_call(
        kern,
        grid=(batch, nq),
        in_specs=[pl.BlockSpec((4, DIFF_QK), lambda b, i: (0, 0)),
                  pl.BlockSpec((1, GROUP), lambda b, i: (0, 0)),
                  pl.BlockSpec((tq, GROUP), lambda b, i: (b * nq + i, 0)),
                  pl.BlockSpec((seq, GROUP), lambda b, i: (b, 0)),
                  pl.BlockSpec((seq, GROUP), lambda b, i: (b, 0))],
        out_specs=pl.BlockSpec((tq, GROUP), lambda b, i: (b * nq + i, 0)),
        out_shape=jax.ShapeDtypeStruct((batch * seq, GROUP), F32),
        scratch_shapes=[pltpu.VMEM((2 * N_HEADS, tq, 128), F32),
                        pltpu.VMEM((2 * N_HEADS, tq, 128), F32),
                        pltpu.VMEM((2, tq, GROUP), F32)],
        compiler_params=_cparams(("parallel", "arbitrary")),
        name="diff_attn",
    )(lam_vecs, jnp.tile(subln_g, N_HEADS)[None, :], aq, ak, av)


def _sgu_kernel(u_ref, v_ref, g_ref, b_ref, w_ref, bt_ref, o_ref, *, tm):
    lane = lax.broadcasted_iota(jnp.int32, (1, GROUP), 1)
    head_of_lane = lane // HEAD_DIM
    u = jax.nn.gelu(u_ref[...])
    vn = _layer_norm_rows(jax.nn.gelu(v_ref[...]), g_ref[...], b_ref[...]).astype(BF16)
    r = lax.broadcasted_iota(jnp.int32, (CHUNK, CHUNK), 0)
    c = lax.broadcasted_iota(jnp.int32, (CHUNK, CHUNK), 1)
    w_all = jnp.concatenate([jnp.where(c <= r, w_ref[g], 0.0) for g in range(N_HEADS)], axis=0).astype(BF16)
    for ch in range(tm // CHUNK):
        rows = slice(ch * CHUNK, (ch + 1) * CHUNK)
        res = jnp.dot(w_all, vn[rows], preferred_element_type=F32)
        zc = _head_select(head_of_lane, [res[g * CHUNK:(g + 1) * CHUNK] for g in range(N_HEADS)])
        o_ref[rows, :] = u[rows] * (zc + bt_ref[...])


def _sgu(z, ln_g, ln_b, w_s, b_s):
    m = z.shape[0]
    tm = 512
    bias_tab = jnp.repeat(jnp.transpose(b_s), HEAD_DIM, axis=1)
    return pl.pallas_call(
        functools.partial(_sgu_kernel, tm=tm),
        grid=(m // tm,),
        in_specs=[pl.BlockSpec((tm, GROUP), lambda i: (i, 3)),
                  pl.BlockSpec((tm, GROUP), lambda i: (i, 4)),
                  pl.BlockSpec((1, GROUP), lambda i: (0, 0)),
                  pl.BlockSpec((1, GROUP), lambda i: (0, 0)),
                  pl.BlockSpec((N_HEADS, CHUNK, CHUNK), lambda i: (0, 0, 0)),
                  pl.BlockSpec((CHUNK, GROUP), lambda i: (0, 0))],
        out_specs=pl.BlockSpec((tm, GROUP), lambda i: (i, 0)),
        out_shape=jax.ShapeDtypeStruct((m, GROUP), F32),
        compiler_params=_cparams(("parallel",)),
        name="sgu",
    )(z, z, ln_g[None, :], ln_b[None, :], w_s, bias_tab)


def _dilated_kernel(*refs):
    q_ref, kc_ref, kp_ref, vc_ref, vp_ref = [refs[2 * t:2 * t + 2] for t in range(5)]
    o_ref = refs[10:12]
    os_ref = [refs[12 + 2 * p:14 + 2 * p] for p in range(3)]
    ls_ref = [refs[18 + 2 * p:20 + 2 * p] for p in range(3)]
    n = pl.program_id(1)
    lane = lax.broadcasted_iota(jnp.int32, (1, GROUP), 1)
    head_of_lane = lane // HEAD_DIM
    i_idx = lax.broadcasted_iota(jnp.int32, (CHUNK, 2 * CHUNK), 0)
    j_idx = lax.broadcasted_iota(jnp.int32, (CHUNK, 2 * CHUNK), 1)
    dist = CHUNK + i_idx - j_idx
    band = (dist >= 0) & (dist <= CHUNK)
    band_first = band & ((j_idx >= CHUNK) | (n > 0))

    def rows(start, d):
        return pl.ds(start, CHUNK) if d == 1 else pl.ds(start, CHUNK, stride=d)

    def strided(pair, start, d):
        return jnp.concatenate([pair[0][rows(start, d), :], pair[1][rows(start, d), :]], axis=1)

    def store(pair, start, d, val):
        pair[0][rows(start, d), :] = val[:, :128]
        pair[1][rows(start, d), :] = val[:, 128:]

    for pi, d in enumerate(DILATIONS):
        nblk = SPAN // (CHUNK * d)
        for r in range(d):
            for nb in range(nblk):
                start = nb * CHUNK * d + r
                qb = strided(q_ref, start, d).astype(BF16)
                if nb == 0:
                    pstart = (nblk - 1) * CHUNK * d + r
                    k_prev, v_prev = strided(kp_ref, pstart, d), strided(vp_ref, pstart, d)
                    valid = band_first
                else:
                    pstart = (nb - 1) * CHUNK * d + r
                    k_prev, v_prev = strided(kc_ref, pstart, d), strided(vc_ref, pstart, d)
                    valid = band
                kk = jnp.concatenate([k_prev, strided(kc_ref, start, d)], axis=0).astype(BF16)
                vv = jnp.concatenate([v_prev, strided(vc_ref, start, d)], axis=0).astype(BF16)
                q_zero = jnp.zeros_like(qb)
                v_zero = jnp.zeros_like(vv)
                probs, lses = [], []
                for h in range(N_HEADS):
                    qh = jnp.where(head_of_lane == h, qb, q_zero)
                    s = lax.dot_general(qh, kk, (((1,), (1,)), ((), ())), preferred_element_type=F32)
                    s = jnp.where(valid, s, NEG_INF)
                    mx = jnp.max(s, axis=1, keepdims=True)
                    p = jnp.exp(s - mx)
                    den = jnp.sum(p, axis=1, keepdims=True)
                    probs.append((p / den).astype(BF16))
                    lses.append(mx + jnp.log(den))
                v_bd = jnp.concatenate([jnp.where(head_of_lane == h, vv, v_zero) for h in range(N_HEADS)], axis=0)
                ob = jnp.dot(jnp.concatenate(probs, axis=1), v_bd, preferred_element_type=F32)
                lb = _head_select(head_of_lane, lses) + jnp.zeros((CHUNK, GROUP), F32)
                store(os_ref[pi], start, d, ob)
                store(ls_ref[pi], start, d, lb)

    for half in range(2):
        l0, l1, l2 = ls_ref[0][half][...], ls_ref[1][half][...], ls_ref[2][half][...]
        mx = jnp.maximum(jnp.maximum(l0, l1), l2)
        w0, w1, w2 = jnp.exp(l0 - mx), jnp.exp(l1 - mx), jnp.exp(l2 - mx)
        tot = w0 + w1 + w2
        o_ref[half][...] = ((w0 / tot) * os_ref[0][half][...] + (w1 / tot) * os_ref[1][half][...]
                            + (w2 / tot) * os_ref[2][half][...])


def _dilated(cq, ck, z, batch, seq):
    ns = seq // SPAN
    blk = (SPAN, 128)

    def halves(col, prev):
        def spec(c):
            if prev:
                return pl.BlockSpec(blk, lambda b, n: (b * ns + jnp.maximum(n - 1, 0), c))
            return pl.BlockSpec(blk, lambda b, n: (b * ns + n, c))
        return [spec(2 * col), spec(2 * col + 1)]

    v_col = 7
    return pl.pallas_call(
        _dilated_kernel,
        grid=(batch, ns),
        in_specs=(halves(0, False) + halves(0, False) + halves(0, True)
                  + halves(v_col, False) + halves(v_col, True)),
        out_specs=[pl.BlockSpec(blk, lambda b, n: (b * ns + n, 0))] * 2,
        out_shape=[jax.ShapeDtypeStruct((batch * seq, 128), F32)] * 2,
        scratch_shapes=[pltpu.VMEM(blk, F32)] * 12,
        compiler_params=_cparams(("parallel", "arbitrary")),
        name="dilated",
    )(cq, cq, ck, ck, ck, ck, z, z, z, z)


def _lane_pick(x, lane, idx):
    return jnp.sum(jnp.where(lane == idx, x, 0.0), axis=1, keepdims=True)


def _mlstm_kernel(q_ref, k_ref, v_ref, og_ref, gt_ref, cw_ref, cb_ref, gb_ref, ng_ref, o_ref,
                  tail_ref, ct_ref, nm_ref, ms_ref):
    c_idx = pl.program_id(1)
    L = CHUNK
    lane = lax.broadcasted_iota(jnp.int32, (1, GROUP), 1)
    head_of_lane = lane // HEAD_DIM
    lane1 = lax.broadcasted_iota(jnp.int32, (1, 128), 1)
    row_head = lax.broadcasted_iota(jnp.int32, (GROUP, 1), 0) // HEAD_DIM

    @pl.when(c_idx == 0)
    def _():
        tail_ref[...] = jnp.zeros(tail_ref.shape, F32)
        ct_ref[...] = jnp.zeros(ct_ref.shape, F32)
        nm_ref[...] = jnp.zeros(nm_ref.shape, F32)
        ms_ref[...] = jnp.zeros(ms_ref.shape, F32)

    x = jnp.concatenate([q_ref[...], k_ref[...]], axis=1)
    tail = tail_ref[...]
    rows = lax.broadcasted_iota(jnp.int32, (L, 1), 0)
    pad = jnp.zeros((L - 8, 2 * GROUP), F32)
    y = x * cw_ref[CONV_WIDTH - 1:CONV_WIDTH, :]
    for back in range(1, CONV_WIDTH):
        shifted = jnp.where(rows >= back, pltpu.roll(x, back, 0),
                            jnp.concatenate([pltpu.roll(tail, back, 0), pad], axis=0))
        y = y + shifted * cw_ref[CONV_WIDTH - 1 - back:CONV_WIDTH - back, :]
    tail_ref[...] = x[L - 8:, :]
    qk = jax.nn.silu(y + cb_ref[...])
    mq = qk[:, :GROUP].astype(BF16)
    mk = (qk[:, GROUP:] * (HEAD_DIM ** -0.5)).astype(BF16)
    v = v_ref[...]
    vb = v.astype(BF16)

    gates = gt_ref[...] + gb_ref[...]
    log_f = jnp.minimum(gates, 0.0) - jnp.log1p(jnp.exp(-jnp.abs(gates)))
    xg = jnp.where(lane1 < N_HEADS, gates, jnp.where(lane1 < 2 * N_HEADS, log_f, 0.0))
    r = lax.broadcasted_iota(jnp.int32, (L, L), 0)
    c = lax.broadcasted_iota(jnp.int32, (L, L), 1)
    tril = c <= r
    csum = jnp.dot(jnp.where(tril, 1.0, 0.0), xg, preferred_element_type=F32,
                   precision=lax.Precision.HIGHEST)
    xg_t = jnp.transpose(xg)
    csum_t = jnp.transpose(csum)
    m_state = ms_ref[...]

    q_zero = jnp.zeros_like(mq)
    sws, e_cols, mout_cols, den_cols, w_cols = [], [], [], [], []
    a_sc, e_sc, m_news = [], [], []
    nq = jnp.dot(mq, nm_ref[...].astype(BF16), preferred_element_type=F32)
    for h in range(N_HEADS):
        b_c = _lane_pick(csum, lane1, N_HEADS + h)
        i_c = _lane_pick(xg, lane1, h)
        b_r = csum_t[N_HEADS + h:N_HEADS + h + 1, :]
        i_r = xg_t[h:h + 1, :]
        b_last = _lane_pick(b_r, lane1, L - 1)
        m_prev = _lane_pick(m_state, lane1, h)
        d_log = jnp.where(tril, b_c - b_r + i_r, NEG_INF)
        g_c = b_last - b_c + i_c
        g_max = jnp.max(g_c, axis=0, keepdims=True)
        w_cols.append(jnp.exp(g_c - g_max))
        inter = b_c + m_prev
        m_out = jnp.maximum(inter, jnp.max(d_log, axis=1, keepdims=True))
        p = jnp.exp(d_log - m_out)
        qh = jnp.where(head_of_lane == h, mq, q_zero)
        sw = p * lax.dot_general(qh, mk, (((1,), (1,)), ((), ())), preferred_element_type=F32)
        e_t = jnp.exp(inter - m_out)
        den_cols.append(jnp.sum(sw, axis=1, keepdims=True) + e_t * _lane_pick(nq, lane1, h))
        sws.append(sw.astype(BF16))
        e_cols.append(e_t)
        mout_cols.append(m_out)
        m_new = jnp.maximum(b_last + m_prev, g_max)
        a_sc.append(jnp.exp(b_last + m_prev - m_new))
        e_sc.append(jnp.exp(g_max - m_new))
        m_news.append(m_new)

    v_zero = jnp.zeros_like(vb)
    v_bd = jnp.concatenate([jnp.where(head_of_lane == h, vb, v_zero) for h in range(N_HEADS)], axis=0)
    num = (jnp.dot(jnp.concatenate(sws, axis=1), v_bd, preferred_element_type=F32)
           + _head_select(head_of_lane, e_cols)
           * jnp.dot(mq, ct_ref[...].astype(BF16), preferred_element_type=F32))
    den = _head_select(head_of_lane, den_cols)
    hid = num / jnp.maximum(jnp.abs(den), jnp.exp(-_head_select(head_of_lane, mout_cols)))

    wv = (_head_select(head_of_lane, w_cols) * v).astype(BF16)
    ct_loc = lax.dot_general(mk, wv, (((0,), (0,)), ((), ())), preferred_element_type=F32)
    ct_loc = jnp.where(row_head == head_of_lane, ct_loc, 0.0)
    ct_ref[...] = _head_select(row_head, a_sc) * ct_ref[...] + _head_select(row_head, e_sc) * ct_loc
    w_mat = jnp.zeros((L, 128), F32)
    for h in range(N_HEADS):
        w_mat = jnp.where(lane1 == h, w_cols[h], w_mat)
    nm_loc = lax.dot_general(mk, w_mat.astype(BF16), (((0,), (0,)), ((), ())), preferred_element_type=F32)
    nm_loc = jnp.where(row_head == lane1, nm_loc, 0.0)
    nm_ref[...] = _head_select(row_head, a_sc) * nm_ref[...] + _head_select(row_head, e_sc) * nm_loc
    m_row = jnp.zeros((1, 128), F32)
    for h in range(N_HEADS):
        m_row = jnp.where(lane1 == h, m_news[h], m_row)
    ms_ref[...] = m_row

    mus = [jnp.sum(jnp.where(head_of_lane == h, hid, 0.0), axis=1, keepdims=True) * (1.0 / HEAD_DIM)
           for h in range(N_HEADS)]
    hc = hid - _head_select(head_of_lane, mus)
    hc2 = hc * hc
    var = [jnp.sum(jnp.where(head_of_lane == h, hc2, 0.0), axis=1, keepdims=True) * (1.0 / HEAD_DIM)
           for h in range(N_HEADS)]
    normed = hc * lax.rsqrt(_head_select(head_of_lane, var) + LN_EPS) * ng_ref[...]
    o_ref[...] = jax.nn.sigmoid(og_ref[...]) * normed


def _mlstm(z, conv_w, conv_b, gate_b, norm_g, batch, seq):
    nc = seq // CHUNK
    row = lambda col: (lambda b, c: (b * nc + c, col))
    const = lambda b, c: (0, 0)
    gate_tab = jnp.zeros((1, 128), F32).at[0, :2 * N_HEADS].set(gate_b.reshape(-1))
    return pl.pallas_call(
        _mlstm_kernel,
        grid=(batch, nc),
        in_specs=[pl.BlockSpec((CHUNK, GROUP), row(8)), pl.BlockSpec((CHUNK, GROUP), row(9)),
                  pl.BlockSpec((CHUNK, GROUP), row(10)), pl.BlockSpec((CHUNK, GROUP), row(11)),
                  pl.BlockSpec((CHUNK, 128), row(GATE_BLOCK)),
                  pl.BlockSpec((CONV_WIDTH, 2 * GROUP), const), pl.BlockSpec((1, 2 * GROUP), const),
                  pl.BlockSpec((1, 128), const), pl.BlockSpec((1, GROUP), const)],
        out_specs=pl.BlockSpec((CHUNK, GROUP), row(0)),
        out_shape=jax.ShapeDtypeStruct((batch * seq, GROUP), F32),
        scratch_shapes=[pltpu.VMEM((8, 2 * GROUP), F32), pltpu.VMEM((GROUP, GROUP), F32),
                        pltpu.VMEM((GROUP, 128), F32), pltpu.VMEM((1, 128), F32)],
        compiler_params=_cparams(("arbitrary", "arbitrary")),
        name="mlstm",
    )(z, z, z, z, z, conv_w, conv_b[None, :], gate_tab, jnp.tile(norm_g, N_HEADS)[None, :])


def _outproj_kernel(a_ref, b_ref, c_lo_ref, c_hi_ref, d_ref, w_ref, x_ref, g_ref, bb_ref, o_ref):
    mixed = jnp.concatenate([a_ref[...], b_ref[...], c_lo_ref[...], c_hi_ref[...], d_ref[...]],
                            axis=1).astype(BF16)
    h = jnp.dot(mixed, w_ref[...], preferred_element_type=F32)
    o_ref[...] = _layer_norm_rows(ALPHA * x_ref[...] + h, g_ref[...], bb_ref[...])


def _outproj(oa, ob, oc, od, w, x2, g, b):
    m = x2.shape[0]
    tm = 512
    part = pl.BlockSpec((tm, GROUP), lambda i: (i, 0))
    half = pl.BlockSpec((tm, 128), lambda i: (i, 0))
    full = pl.BlockSpec((tm, D_MODEL), lambda i: (i, 0))
    vec = pl.BlockSpec((1, D_MODEL), lambda i: (0, 0))
    return pl.pallas_call(
        _outproj_kernel,
        grid=(m // tm,),
        in_specs=[part, part, half, half, part, pl.BlockSpec((D_MODEL, D_MODEL), lambda i: (0, 0)),
                  full, vec, vec],
        out_specs=full,
        out_shape=jax.ShapeDtypeStruct((m, D_MODEL), F32),
        compiler_params=_cparams(("parallel",)),
        name="outproj",
    )(oa, ob, oc[0], oc[1], od, w, x2, g[None, :], b[None, :])


def _ffn_kernel(x_ref, wg_ref, wu_ref, wd_ref, g_ref, b_ref, o_ref, acc_ref):
    j = pl.program_id(1)
    xb = x_ref[...].astype(BF16)
    gate = jnp.dot(xb, wg_ref[...], preferred_element_type=F32)
    up = jnp.dot(xb, wu_ref[...], preferred_element_type=F32)
    part = jnp.dot((jax.nn.silu(gate) * up).astype(BF16), wd_ref[...], preferred_element_type=F32)

    @pl.when(j == 0)
    def _():
        acc_ref[...] = part

    @pl.when(j > 0)
    def _():
        acc_ref[...] += part

    @pl.when(j == pl.num_programs(1) - 1)
    def _():
        o_ref[...] = _layer_norm_rows(ALPHA * x_ref[...] + acc_ref[...], g_ref[...], b_ref[...])


def _ffn(x2, wg, wu, wd, g, b):
    m = x2.shape[0]
    tm, tf = 512, 1408
    full = pl.BlockSpec((tm, D_MODEL), lambda i, j: (i, 0))
    vec = pl.BlockSpec((1, D_MODEL), lambda i, j: (0, 0))
    return pl.pallas_call(
        _ffn_kernel,
        grid=(m // tm, D_FF // tf),
        in_specs=[full, pl.BlockSpec((D_MODEL, tf), lambda i, j: (0, j)),
                  pl.BlockSpec((D_MODEL, tf), lambda i, j: (0, j)),
                  pl.BlockSpec((tf, D_MODEL), lambda i, j: (j, 0)), vec, vec],
        out_specs=full,
        out_shape=jax.ShapeDtypeStruct((m, D_MODEL), F32),
        scratch_shapes=[pltpu.VMEM((tm, D_MODEL), F32)],
        compiler_params=_cparams(("parallel", "arbitrary")),
        name="ffn",
    )(x2, wg, wu, wd, g[None, :], b[None, :])


def kernel(x, w_in, diff_lambda, diff_subln_g, sgu_ln_g, sgu_ln_b, sgu_w, sgu_b, mlstm_conv_w, mlstm_conv_b,
           mlstm_gate_b, mlstm_norm_g, w_out, ln1_g, ln1_b, w_gate, w_up, w_down, ln2_g, ln2_b):
    batch, seq, _ = x.shape
    assert seq % SPAN == 0
    x2 = x.reshape(batch * seq, D_MODEL)
    tab_a = _rope_table(seq, DIFF_QK, GROUP // DIFF_QK)
    tab_c = _rope_table(seq, HEAD_DIM, N_HEADS)
    w_in_p = jnp.pad(w_in, ((0, 0), (0, 0), (0, IN_PAD - IN_WIDTH))).astype(BF16)
    w_out_b = w_out.astype(BF16)
    w_gate_b, w_up_b, w_down_b = w_gate.astype(BF16), w_up.astype(BF16), w_down.astype(BF16)
    for l in range(DEPTH):
        lam_init = 0.8 - 0.6 * math.exp(-0.3 * l)
        z = _inproj(x2, w_in_p[l])
        aq, ak, av, cq, ck = _prep(z, tab_a, tab_c, seq)
        out_a = _diff_attn(aq, ak, av, diff_lambda[l], diff_subln_g[l], lam_init, batch, seq)
        out_b = _sgu(z, sgu_ln_g[l], sgu_ln_b[l], sgu_w[l], sgu_b[l])
        out_c = _dilated(cq, ck, z, batch, seq)
        out_d = _mlstm(z, mlstm_conv_w[l], mlstm_conv_b[l], mlstm_gate_b[l], mlstm_norm_g[l], batch, seq)
        x2 = _outproj(out_a, out_b, out_c, out_d, w_out_b[l], x2, ln1_g[l], ln1_b[l])
        x2 = _ffn(x2, w_gate_b[l], w_up_b[l], w_down_b[l], ln2_g[l], ln2_b[l])
    return x2.reshape(batch, seq, D_MODEL)
```

```python
import functools
import math

import jax
import jax.numpy as jnp
from jax import lax
from jax.experimental import pallas as pl
from jax.experimental.pallas import tpu as pltpu

F32 = jnp.float32
BF16 = jnp.bfloat16

D_MODEL = 1024
DEPTH = 4
N_HEADS = 4
HEAD_DIM = 64
GROUP = N_HEADS * HEAD_DIM
DIFF_QK = HEAD_DIM // 2
CHUNK = 128
DILATIONS = (1, 4, 16)
SPAN = CHUNK * DILATIONS[-1]
CONV_WIDTH = 4
ROPE_THETA = 500000.0
D_FF = 2816
FF_CHUNK = 256
IN_WIDTH = 3080
IN_PAD = 3200
IN_CHUNK = 640
GATE_BLOCK = 3072 // 128
ATT_TILE = 256
ACC_ROWS = HEAD_DIM + 16
LOG2E = math.log2(math.e)
ALPHA = (2 * DEPTH) ** 0.25
LN_EPS = 1e-5
NEG_INF = float("-inf")
VMEM_LIMIT = 56 * 1024 * 1024


def _cparams(sem):
    return pltpu.CompilerParams(dimension_semantics=sem, vmem_limit_bytes=VMEM_LIMIT)


def _head_select(head_of_lane, parts):
    out = parts[-1]
    for h in range(len(parts) - 2, -1, -1):
        out = jnp.where(head_of_lane == h, parts[h], out)
    return out


def _layer_norm_rows(y, g, b):
    mu = jnp.mean(y, axis=-1, keepdims=True)
    yc = y - mu
    var = jnp.mean(yc * yc, axis=-1, keepdims=True)
    return yc * lax.rsqrt(var + LN_EPS) * g + b


def _inproj_kernel(x_ref, w_ref, o_ref):
    xb = x_ref[...].astype(BF16)
    for c in range(IN_PAD // IN_CHUNK):
        cols = slice(c * IN_CHUNK, (c + 1) * IN_CHUNK)
        o_ref[:, cols] = jnp.dot(xb, w_ref[:, cols], preferred_element_type=F32)


def _inproj(x2, w):
    m = x2.shape[0]
    tm = 512
    return pl.pallas_call(
        _inproj_kernel,
        grid=(m // tm,),
        in_specs=[pl.BlockSpec((tm, D_MODEL), lambda i: (i, 0)),
                  pl.BlockSpec((D_MODEL, IN_PAD), lambda i: (0, 0))],
        out_specs=pl.BlockSpec((tm, IN_PAD), lambda i: (i, 0)),
        out_shape=jax.ShapeDtypeStruct((m, IN_PAD), F32),
        compiler_params=_cparams(("parallel",)),
        name="inproj",
    )(x2, w)


def _rope(x, c, s_lo, s_hi, half):
    width = x.shape[1]
    return x * c + pltpu.roll(x, width - half, 1) * s_lo + pltpu.roll(x, half, 1) * s_hi


def _prep_kernel(aq_ref, ak_ref, av_ref, cq_ref, ck_ref, ta_ref, tc_ref,
                 oaq_ref, oak_ref, oav_ref, ocq_ref, ock_ref):
    ta_c, ta_lo, ta_hi = ta_ref[0], ta_ref[1], ta_ref[2]
    tc_c, tc_lo, tc_hi = tc_ref[0], tc_ref[1], tc_ref[2]
    half_a = DIFF_QK // 8
    half_c = HEAD_DIM // 8
    q = _rope(aq_ref[...], ta_c, ta_lo, ta_hi, half_a) * (DIFF_QK ** -0.5 * LOG2E)
    v = av_ref[...]
    for t in range(q.shape[0] // ATT_TILE):
        rows = slice(t * ATT_TILE, (t + 1) * ATT_TILE)
        oaq_ref[t] = jnp.transpose(q[rows])
        oav_ref[t] = jnp.transpose(v[rows]).astype(BF16)
    oak_ref[...] = _rope(ak_ref[...], ta_c, ta_lo, ta_hi, half_a).astype(BF16)
    ocq_ref[...] = _rope(cq_ref[...], tc_c, tc_lo, tc_hi, half_c) * (HEAD_DIM ** -0.5)
    ock_ref[...] = _rope(ck_ref[...], tc_c, tc_lo, tc_hi, half_c)


def _prep(z, tab_a, tab_c, seq):
    m = z.shape[0]
    tm = 512
    nt = seq // tm
    zspec = lambda c: pl.BlockSpec((tm, GROUP), lambda i, c=c: (i, c))
    tspec = pl.BlockSpec((3, tm, GROUP), lambda i: (0, i % nt, 0))
    ospec = pl.BlockSpec((tm, GROUP), lambda i: (i, 0))
    tspec_out = pl.BlockSpec((tm // ATT_TILE, GROUP, ATT_TILE), lambda i: (i, 0, 0))
    transposed = lambda dt: jax.ShapeDtypeStruct((m // ATT_TILE, GROUP, ATT_TILE), dt)
    return pl.pallas_call(
        _prep_kernel,
        grid=(m // tm,),
        in_specs=[zspec(0), zspec(1), zspec(2), zspec(5), zspec(6), tspec, tspec],
        out_specs=[tspec_out, ospec, tspec_out, ospec, ospec],
        out_shape=[transposed(F32), jax.ShapeDtypeStruct((m, GROUP), BF16), transposed(BF16),
                   jax.ShapeDtypeStruct((m, GROUP), F32), jax.ShapeDtypeStruct((m, GROUP), F32)],
        compiler_params=_cparams(("parallel",)),
        name="prep",
    )(z, z, z, z, z, tab_a, tab_c)


def _rope_table(seq, group_dim, n_groups):
    rot = group_dim // 4
    half = rot // 2
    pos = jnp.arange(seq, dtype=F32)
    inv = ROPE_THETA ** (-jnp.arange(0, rot, 2, dtype=F32) / rot)
    ang = pos[:, None] * inv[None, :]
    cos, sin = jnp.cos(ang), jnp.sin(ang)
    ones = jnp.ones((seq, group_dim - rot), F32)
    zeros_half = jnp.zeros((seq, half), F32)
    zeros_rest = jnp.zeros((seq, group_dim - rot), F32)
    c = jnp.concatenate([cos, cos, ones], axis=1)
    lo = jnp.concatenate([-sin, zeros_half, zeros_rest], axis=1)
    hi = jnp.concatenate([zeros_half, sin, zeros_rest], axis=1)
    return jnp.stack([jnp.tile(t, (1, n_groups)) for t in (c, lo, hi)], axis=0)


def _diff_attn_kernel(lamv_ref, g_ref, qt_ref, k_ref, vt_ref, o_ref, qm_ref, m_ref, acc_ref, sa_ref, sb_ref,
                      *, lam_init):
    t = ATT_TILE
    i = pl.program_id(1)
    map_of_row = lax.broadcasted_iota(jnp.int32, (GROUP, 1), 0) // DIFF_QK
    qt = qt_ref[...]
    for hm in range(2 * N_HEADS):
        qm_ref[hm] = jnp.where(map_of_row == hm, qt, 0.0).astype(BF16)
    m_ref[...] = jnp.full(m_ref.shape, NEG_INF, F32)
    acc_ref[...] = jnp.zeros(acc_ref.shape, F32)
    ones = jnp.ones((ACC_ROWS - HEAD_DIM, t), BF16)

    def key_tile(j):
        return k_ref[pl.ds(pl.multiple_of(j * t, t), t), :]

    def consume(src_ref, j, hm, masked):
        s = src_ref[hm]
        if masked:
            key_pos = j * t + lax.broadcasted_iota(jnp.int32, (t, t), 0)
            q_pos = i * t + lax.broadcasted_iota(jnp.int32, (t, t), 1)
            s = jnp.where(key_pos <= q_pos, s, NEG_INF)
        m_prev = m_ref[hm:hm + 1, :]
        m_new = jnp.maximum(m_prev, jnp.max(s, axis=0, keepdims=True))
        alpha = jnp.exp2(m_prev - m_new)
        p = jnp.exp2(s - m_new).astype(BF16)
        h = hm // 2
        v_ext = jnp.concatenate([vt_ref[j, h * HEAD_DIM:(h + 1) * HEAD_DIM, :], ones], axis=0)
        acc_ref[hm] = alpha * acc_ref[hm] + jnp.dot(v_ext, p, preferred_element_type=F32)
        m_ref[hm:hm + 1, :] = m_new

    def stage(src_ref, j, dst_ref, masked):
        k_next = None if dst_ref is None else key_tile(j + 1)
        for hm in range(2 * N_HEADS):
            if dst_ref is not None:
                dst_ref[hm] = jnp.dot(k_next, qm_ref[hm], preferred_element_type=F32)
            consume(src_ref, j, hm, masked)

    k0 = key_tile(0)
    for hm in range(2 * N_HEADS):
        sa_ref[hm] = jnp.dot(k0, qm_ref[hm], preferred_element_type=F32)

    n_pairs = (i + 2) // 2
    n_plain = n_pairs - 1

    def pair(first):
        stage(sa_ref, first, sb_ref, False)
        stage(sb_ref, first + 1, sa_ref, False)

    def two_pairs(u, carry):
        pair(4 * u)
        pair(4 * u + 2)
        return carry

    def one_pair(u, carry):
        pair(2 * u)
        return carry

    lax.fori_loop(0, n_plain // 2, two_pairs, 0)
    lax.fori_loop(2 * (n_plain // 2), n_plain, one_pair, 0)
    last = 2 * n_plain
    stage(sa_ref, last, sb_ref, True)
    stage(sb_ref, last + 1, None, True)

    lv = lamv_ref[...]
    lam = (jnp.exp(jnp.sum(lv[0:1] * lv[1:2], axis=1, keepdims=True))
           - jnp.exp(jnp.sum(lv[2:3] * lv[3:4], axis=1, keepdims=True)) + lam_init)
    outs = []
    for h in range(N_HEADS):
        a0, a1 = acc_ref[2 * h], acc_ref[2 * h + 1]
        a = a0[:HEAD_DIM] / a0[HEAD_DIM:HEAD_DIM + 1] - lam * (a1[:HEAD_DIM] / a1[HEAD_DIM:HEAD_DIM + 1])
        ms = jnp.mean(a * a, axis=0, keepdims=True)
        outs.append(a * lax.rsqrt(ms + LN_EPS))
    o_ref[...] = jnp.transpose(jnp.concatenate(outs, axis=0)) * g_ref[...] * (1.0 - lam_init)


def _diff_attn(aq_t, ak, av_t, lam_vecs, subln_g, lam_init, batch, seq):
    t = ATT_TILE
    nq = seq // t
    return pl.pallas_call(
        functools.partial(_diff_attn_kernel, lam_init=lam_init),
        grid=(batch, nq),
        in_specs=[pl.BlockSpec((4, DIFF_QK), lambda b, i: (0, 0)),
                  pl.BlockSpec((1, GROUP), lambda b, i: (0, 0)),
                  pl.BlockSpec((None, GROUP, t), lambda b, i: (b * nq + i, 0, 0)),
                  pl.BlockSpec((seq, GROUP), lambda b, i: (b, 0)),
                  pl.BlockSpec((nq, GROUP, t), lambda b, i: (b, 0, 0))],
        out_specs=pl.BlockSpec((t, GROUP), lambda b, i: (b * nq + i, 0)),
        out_shape=jax.ShapeDtypeStruct((batch * seq, GROUP), F32),
        scratch_shapes=[pltpu.VMEM((2 * N_HEADS, GROUP, t), BF16),
                        pltpu.VMEM((2 * N_HEADS, t), F32),
                        pltpu.VMEM((2 * N_HEADS, ACC_ROWS, t), F32),
                        pltpu.VMEM((2 * N_HEADS, t, t), F32),
                        pltpu.VMEM((2 * N_HEADS, t, t), F32)],
        compiler_params=_cparams(("parallel", "arbitrary")),
        name="diff_attn",
    )(lam_vecs, jnp.tile(subln_g, N_HEADS)[None, :], aq_t, ak, av_t)


def _sgu_kernel(u_ref, v_ref, g_ref, b_ref, w_ref, bt_ref, o_ref, *, tm):
    lane = lax.broadcasted_iota(jnp.int32, (1, GROUP), 1)
    head_of_lane = lane // HEAD_DIM
    u = jax.nn.gelu(u_ref[...])
    vn = _layer_norm_rows(jax.nn.gelu(v_ref[...]), g_ref[...], b_ref[...]).astype(BF16)
    r = lax.broadcasted_iota(jnp.int32, (CHUNK, CHUNK), 0)
    c = lax.broadcasted_iota(jnp.int32, (CHUNK, CHUNK), 1)
    w_all = jnp.concatenate([jnp.where(c <= r, w_ref[g], 0.0) for g in range(N_HEADS)], axis=0).astype(BF16)
    for ch in range(tm // CHUNK):
        rows = slice(ch * CHUNK, (ch + 1) * CHUNK)
        res = jnp.dot(w_all, vn[rows], preferred_element_type=F32)
        zc = _head_select(head_of_lane, [res[g * CHUNK:(g + 1) * CHUNK] for g in range(N_HEADS)])
        o_ref[rows, :] = u[rows] * (zc + bt_ref[...])


def _sgu(z, ln_g, ln_b, w_s, b_s):
    m = z.shape[0]
    tm = 512
    bias_tab = jnp.repeat(jnp.transpose(b_s), HEAD_DIM, axis=1)
    return pl.pallas_call(
        functools.partial(_sgu_kernel, tm=tm),
        grid=(m // tm,),
        in_specs=[pl.BlockSpec((tm, GROUP), lambda i: (i, 3)),
                  pl.BlockSpec((tm, GROUP), lambda i: (i, 4)),
                  pl.BlockSpec((1, GROUP), lambda i: (0, 0)),
                  pl.BlockSpec((1, GROUP), lambda i: (0, 0)),
                  pl.BlockSpec((N_HEADS, CHUNK, CHUNK), lambda i: (0, 0, 0)),
                  pl.BlockSpec((CHUNK, GROUP), lambda i: (0, 0))],
        out_specs=pl.BlockSpec((tm, GROUP), lambda i: (i, 0)),
        out_shape=jax.ShapeDtypeStruct((m, GROUP), F32),
        compiler_params=_cparams(("parallel",)),
        name="sgu",
    )(z, z, ln_g[None, :], ln_b[None, :], w_s, bias_tab)


def _dilated_kernel(*refs):
    q_ref, kc_ref, kp_ref, vc_ref, vp_ref = [refs[2 * t:2 * t + 2] for t in range(5)]
    o_ref = refs[10:12]
    os_ref = [refs[12 + 2 * p:14 + 2 * p] for p in range(3)]
    ls_ref = [refs[18 + 2 * p:20 + 2 * p] for p in range(3)]
    n = pl.program_id(1)
    lane = lax.broadcasted_iota(jnp.int32, (1, GROUP), 1)
    head_of_lane = lane // HEAD_DIM
    i_idx = lax.broadcasted_iota(jnp.int32, (CHUNK, 2 * CHUNK), 0)
    j_idx = lax.broadcasted_iota(jnp.int32, (CHUNK, 2 * CHUNK), 1)
    dist = CHUNK + i_idx - j_idx
    band = (dist >= 0) & (dist <= CHUNK)
    band_first = band & ((j_idx >= CHUNK) | (n > 0))

    def rows(start, d):
        return pl.ds(start, CHUNK) if d == 1 else pl.ds(start, CHUNK, stride=d)

    def strided(pair, start, d):
        return jnp.concatenate([pair[0][rows(start, d), :], pair[1][rows(start, d), :]], axis=1)

    def store(pair, start, d, val):
        pair[0][rows(start, d), :] = val[:, :128]
        pair[1][rows(start, d), :] = val[:, 128:]

    for pi, d in enumerate(DILATIONS):
        nblk = SPAN // (CHUNK * d)
        for r in range(d):
            for nb in range(nblk):
                start = nb * CHUNK * d + r
                qb = strided(q_ref, start, d).astype(BF16)
                if nb == 0:
                    pstart = (nblk - 1) * CHUNK * d + r
                    k_prev, v_prev = strided(kp_ref, pstart, d), strided(vp_ref, pstart, d)
                    valid = band_first
                else:
                    pstart = (nb - 1) * CHUNK * d + r
                    k_prev, v_prev = strided(kc_ref, pstart, d), strided(vc_ref, pstart, d)
                    valid = band
                kk = jnp.concatenate([k_prev, strided(kc_ref, start, d)], axis=0).astype(BF16)
                vv = jnp.concatenate([v_prev, strided(vc_ref, start, d)], axis=0).astype(BF16)
                q_zero = jnp.zeros_like(qb)
                v_zero = jnp.zeros_like(vv)
                probs, lses = [], []
                for h in range(N_HEADS):
                    qh = jnp.where(head_of_lane == h, qb, q_zero)
                    s = lax.dot_general(qh, kk, (((1,), (1,)), ((), ())), preferred_element_type=F32)
                    s = jnp.where(valid, s, NEG_INF)
                    mx = jnp.max(s, axis=1, keepdims=True)
                    p = jnp.exp(s - mx)
                    den = jnp.sum(p, axis=1, keepdims=True)
                    probs.append((p / den).astype(BF16))
                    lses.append(mx + jnp.log(den))
                v_bd = jnp.concatenate([jnp.where(head_of_lane == h, vv, v_zero) for h in range(N_HEADS)], axis=0)
                ob = jnp.dot(jnp.concatenate(probs, axis=1), v_bd, preferred_element_type=F32)
                lb = _head_select(head_of_lane, lses) + jnp.zeros((CHUNK, GROUP), F32)
                store(os_ref[pi], start, d, ob)
                store(ls_ref[pi], start, d, lb)

    for half in range(2):
        l0, l1, l2 = ls_ref[0][half][...], ls_ref[1][half][...], ls_ref[2][half][...]
        mx = jnp.maximum(jnp.maximum(l0, l1), l2)
        w0, w1, w2 = jnp.exp(l0 - mx), jnp.exp(l1 - mx), jnp.exp(l2 - mx)
        tot = w0 + w1 + w2
        o_ref[half][...] = ((w0 / tot) * os_ref[0][half][...] + (w1 / tot) * os_ref[1][half][...]
                            + (w2 / tot) * os_ref[2][half][...])


def _dilated(cq, ck, z, batch, seq):
    ns = seq // SPAN
    blk = (SPAN, 128)

    def halves(col, prev):
        def spec(c):
            if prev:
                return pl.BlockSpec(blk, lambda b, n: (b * ns + jnp.maximum(n - 1, 0), c))
            return pl.BlockSpec(blk, lambda b, n: (b * ns + n, c))
        return [spec(2 * col), spec(2 * col + 1)]

    v_col = 7
    return pl.pallas_call(
        _dilated_kernel,
        grid=(batch, ns),
        in_specs=(halves(0, False) + halves(0, False) + halves(0, True)
                  + halves(v_col, False) + halves(v_col, True)),
        out_specs=[pl.BlockSpec(blk, lambda b, n: (b * ns + n, 0))] * 2,
        out_shape=[jax.ShapeDtypeStruct((batch * seq, 128), F32)] * 2,
        scratch_shapes=[pltpu.VMEM(blk, F32)] * 12,
        compiler_params=_cparams(("parallel", "arbitrary")),
        name="dilated",
    )(cq, cq, ck, ck, ck, ck, z, z, z, z)


def _lane_pick(x, lane, idx):
    return jnp.sum(jnp.where(lane == idx, x, 0.0), axis=1, keepdims=True)


def _mlstm_kernel(q_ref, k_ref, v_ref, og_ref, gt_ref, cw_ref, cb_ref, gb_ref, ng_ref, o_ref,
                  tail_ref, ct_ref, nm_ref, ms_ref):
    c_idx = pl.program_id(1)
    L = CHUNK
    lane = lax.broadcasted_iota(jnp.int32, (1, GROUP), 1)
    head_of_lane = lane // HEAD_DIM
    lane1 = lax.broadcasted_iota(jnp.int32, (1, 128), 1)
    row_head = lax.broadcasted_iota(jnp.int32, (GROUP, 1), 0) // HEAD_DIM

    @pl.when(c_idx == 0)
    def _():
        tail_ref[...] = jnp.zeros(tail_ref.shape, F32)
        ct_ref[...] = jnp.zeros(ct_ref.shape, F32)
        nm_ref[...] = jnp.zeros(nm_ref.shape, F32)
        ms_ref[...] = jnp.zeros(ms_ref.shape, F32)

    x = jnp.concatenate([q_ref[...], k_ref[...]], axis=1)
    tail = tail_ref[...]
    rows = lax.broadcasted_iota(jnp.int32, (L, 1), 0)
    pad = jnp.zeros((L - 8, 2 * GROUP), F32)
    y = x * cw_ref[CONV_WIDTH - 1:CONV_WIDTH, :]
    for back in range(1, CONV_WIDTH):
        shifted = jnp.where(rows >= back, pltpu.roll(x, back, 0),
                            jnp.concatenate([pltpu.roll(tail, back, 0), pad], axis=0))
        y = y + shifted * cw_ref[CONV_WIDTH - 1 - back:CONV_WIDTH - back, :]
    tail_ref[...] = x[L - 8:, :]
    qk = jax.nn.silu(y + cb_ref[...])
    mq = qk[:, :GROUP].astype(BF16)
    mk = (qk[:, GROUP:] * (HEAD_DIM ** -0.5)).astype(BF16)
    v = v_ref[...]
    vb = v.astype(BF16)

    gates = gt_ref[...] + gb_ref[...]
    log_f = jnp.minimum(gates, 0.0) - jnp.log1p(jnp.exp(-jnp.abs(gates)))
    xg = jnp.where(lane1 < N_HEADS, gates, jnp.where(lane1 < 2 * N_HEADS, log_f, 0.0))
    r = lax.broadcasted_iota(jnp.int32, (L, L), 0)
    c = lax.broadcasted_iota(jnp.int32, (L, L), 1)
    tril = c <= r
    csum = jnp.dot(jnp.where(tril, 1.0, 0.0), xg, preferred_element_type=F32,
                   precision=lax.Precision.HIGHEST)
    xg_t = jnp.transpose(xg)
    csum_t = jnp.transpose(csum)
    m_state = ms_ref[...]

    q_zero = jnp.zeros_like(mq)
    sws, e_cols, mout_cols, den_cols, w_cols = [], [], [], [], []
    a_sc, e_sc, m_news = [], [], []
    nq = jnp.dot(mq, nm_ref[...].astype(BF16), preferred_element_type=F32)
    for h in range(N_HEADS):
        b_c = _lane_pick(csum, lane1, N_HEADS + h)
        i_c = _lane_pick(xg, lane1, h)
        b_r = csum_t[N_HEADS + h:N_HEADS + h + 1, :]
        i_r = xg_t[h:h + 1, :]
        b_last = _lane_pick(b_r, lane1, L - 1)
        m_prev = _lane_pick(m_state, lane1, h)
        d_log = jnp.where(tril, b_c - b_r + i_r, NEG_INF)
        g_c = b_last - b_c + i_c
        g_max = jnp.max(g_c, axis=0, keepdims=True)
        w_cols.append(jnp.exp(g_c - g_max))
        inter = b_c + m_prev
        m_out = jnp.maximum(inter, jnp.max(d_log, axis=1, keepdims=True))
        p = jnp.exp(d_log - m_out)
        qh = jnp.where(head_of_lane == h, mq, q_zero)
        sw = p * lax.dot_general(qh, mk, (((1,), (1,)), ((), ())), preferred_element_type=F32)
        e_t = jnp.exp(inter - m_out)
        den_cols.append(jnp.sum(sw, axis=1, keepdims=True) + e_t * _lane_pick(nq, lane1, h))
        sws.append(sw.astype(BF16))
        e_cols.append(e_t)
        mout_cols.append(m_out)
        m_new = jnp.maximum(b_last + m_prev, g_max)
        a_sc.append(jnp.exp(b_last + m_prev - m_new))
        e_sc.append(jnp.exp(g_max - m_new))
        m_news.append(m_new)

    v_zero = jnp.zeros_like(vb)
    v_bd = jnp.concatenate([jnp.where(head_of_lane == h, vb, v_zero) for h in range(N_HEADS)], axis=0)
    num = (jnp.dot(jnp.concatenate(sws, axis=1), v_bd, preferred_element_type=F32)
           + _head_select(head_of_lane, e_cols)
           * jnp.dot(mq, ct_ref[...].astype(BF16), preferred_element_type=F32))
    den = _head_select(head_of_lane, den_cols)
    hid = num / jnp.maximum(jnp.abs(den), jnp.exp(-_head_select(head_of_lane, mout_cols)))

    wv = (_head_select(head_of_lane, w_cols) * v).astype(BF16)
    ct_loc = lax.dot_general(mk, wv, (((0,), (0,)), ((), ())), preferred_element_type=F32)
    ct_loc = jnp.where(row_head == head_of_lane, ct_loc, 0.0)
    ct_ref[...] = _head_select(row_head, a_sc) * ct_ref[...] + _head_select(row_head, e_sc) * ct_loc
    w_mat = jnp.zeros((L, 128), F32)
    for h in range(N_HEADS):
        w_mat = jnp.where(lane1 == h, w_cols[h], w_mat)
    nm_loc = lax.dot_general(mk, w_mat.astype(BF16), (((0,), (0,)), ((), ())), preferred_element_type=F32)
    nm_loc = jnp.where(row_head == lane1, nm_loc, 0.0)
    nm_ref[...] = _head_select(row_head, a_sc) * nm_ref[...] + _head_select(row_head, e_sc) * nm_loc
    m_row = jnp.zeros((1, 128), F32)
    for h in range(N_HEADS):
        m_row = jnp.where(lane1 == h, m_news[h], m_row)
    ms_ref[...] = m_row

    mus = [jnp.sum(jnp.where(head_of_lane == h, hid, 0.0), axis=1, keepdims=True) * (1.0 / HEAD_DIM)
           for h in range(N_HEADS)]
    hc = hid - _head_select(head_of_lane, mus)
    hc2 = hc * hc
    var = [jnp.sum(jnp.where(head_of_lane == h, hc2, 0.0), axis=1, keepdims=True) * (1.0 / HEAD_DIM)
           for h in range(N_HEADS)]
    normed = hc * lax.rsqrt(_head_select(head_of_lane, var) + LN_EPS) * ng_ref[...]
    o_ref[...] = jax.nn.sigmoid(og_ref[...]) * normed


def _mlstm(z, conv_w, conv_b, gate_b, norm_g, batch, seq):
    nc = seq // CHUNK
    row = lambda col: (lambda b, c: (b * nc + c, col))
    const = lambda b, c: (0, 0)
    gate_tab = jnp.zeros((1, 128), F32).at[0, :2 * N_HEADS].set(gate_b.reshape(-1))
    return pl.pallas_call(
        _mlstm_kernel,
        grid=(batch, nc),
        in_specs=[pl.BlockSpec((CHUNK, GROUP), row(8)), pl.BlockSpec((CHUNK, GROUP), row(9)),
                  pl.BlockSpec((CHUNK, GROUP), row(10)), pl.BlockSpec((CHUNK, GROUP), row(11)),
                  pl.BlockSpec((CHUNK, 128), row(GATE_BLOCK)),
                  pl.BlockSpec((CONV_WIDTH, 2 * GROUP), const), pl.BlockSpec((1, 2 * GROUP), const),
                  pl.BlockSpec((1, 128), const), pl.BlockSpec((1, GROUP), const)],
        out_specs=pl.BlockSpec((CHUNK, GROUP), row(0)),
        out_shape=jax.ShapeDtypeStruct((batch * seq, GROUP), F32),
        scratch_shapes=[pltpu.VMEM((8, 2 * GROUP), F32), pltpu.VMEM((GROUP, GROUP), F32),
                        pltpu.VMEM((GROUP, 128), F32), pltpu.VMEM((1, 128), F32)],
        compiler_params=_cparams(("arbitrary", "arbitrary")),
        name="mlstm",
    )(z, z, z, z, z, conv_w, conv_b[None, :], gate_tab, jnp.tile(norm_g, N_HEADS)[None, :])


def _outproj_kernel(a_ref, b_ref, c_lo_ref, c_hi_ref, d_ref, w_ref, x_ref, g_ref, bb_ref, o_ref):
    mixed = jnp.concatenate([a_ref[...], b_ref[...], c_lo_ref[...], c_hi_ref[...], d_ref[...]],
                            axis=1).astype(BF16)
    h = jnp.dot(mixed, w_ref[...], preferred_element_type=F32)
    o_ref[...] = _layer_norm_rows(ALPHA * x_ref[...] + h, g_ref[...], bb_ref[...])


def _outproj(oa, ob, oc, od, w, x2, g, b):
    m = x2.shape[0]
    tm = 512
    part = pl.BlockSpec((tm, GROUP), lambda i: (i, 0))
    half = pl.BlockSpec((tm, 128), lambda i: (i, 0))
    full = pl.BlockSpec((tm, D_MODEL), lambda i: (i, 0))
    vec = pl.BlockSpec((1, D_MODEL), lambda i: (0, 0))
    return pl.pallas_call(
        _outproj_kernel,
        grid=(m // tm,),
        in_specs=[part, part, half, half, part, pl.BlockSpec((D_MODEL, D_MODEL), lambda i: (0, 0)),
                  full, vec, vec],
        out_specs=full,
        out_shape=jax.ShapeDtypeStruct((m, D_MODEL), F32),
        compiler_params=_cparams(("parallel",)),
        name="outproj",
    )(oa, ob, oc[0], oc[1], od, w, x2, g[None, :], b[None, :])


def _ffn_kernel(x_ref, wg_ref, wu_ref, wd_ref, g_ref, b_ref, o_ref):
    x = x_ref[...]
    xb = x.astype(BF16)
    hidden = []
    for c in range(D_FF // FF_CHUNK):
        cols = slice(c * FF_CHUNK, (c + 1) * FF_CHUNK)
        gate = jnp.dot(xb, wg_ref[:, cols], preferred_element_type=F32)
        up = jnp.dot(xb, wu_ref[:, cols], preferred_element_type=F32)
        hidden.append((jax.nn.silu(gate) * up).astype(BF16))
    f = jnp.dot(jnp.concatenate(hidden, axis=1), wd_ref[...], preferred_element_type=F32)
    o_ref[...] = _layer_norm_rows(ALPHA * x + f, g_ref[...], b_ref[...])


def _ffn(x2, wg, wu, wd, g, b):
    m = x2.shape[0]
    tm = 512
    full = pl.BlockSpec((tm, D_MODEL), lambda i: (i, 0))
    vec = pl.BlockSpec((1, D_MODEL), lambda i: (0, 0))
    return pl.pallas_call(
        _ffn_kernel,
        grid=(m // tm,),
        in_specs=[full, pl.BlockSpec((D_MODEL, D_FF), lambda i: (0, 0)),
                  pl.BlockSpec((D_MODEL, D_FF), lambda i: (0, 0)),
                  pl.BlockSpec((D_FF, D_MODEL), lambda i: (0, 0)), vec, vec],
        out_specs=full,
        out_shape=jax.ShapeDtypeStruct((m, D_MODEL), F32),
        compiler_params=_cparams(("parallel",)),
        name="ffn",
    )(x2, wg, wu, wd, g[None, :], b[None, :])


def kernel(x, w_in, diff_lambda, diff_subln_g, sgu_ln_g, sgu_ln_b, sgu_w, sgu_b, mlstm_conv_w, mlstm_conv_b,
           mlstm_gate_b, mlstm_norm_g, w_out, ln1_g, ln1_b, w_gate, w_up, w_down, ln2_g, ln2_b):
    batch, seq, _ = x.shape
    assert seq % SPAN == 0
    x2 = x.reshape(batch * seq, D_MODEL)
    tab_a = _rope_table(seq, DIFF_QK, GROUP // DIFF_QK)
    tab_c = _rope_table(seq, HEAD_DIM, N_HEADS)
    w_in_p = jnp.pad(w_in, ((0, 0), (0, 0), (0, IN_PAD - IN_WIDTH))).astype(BF16)
    w_out_b = w_out.astype(BF16)
    w_gate_b, w_up_b, w_down_b = w_gate.astype(BF16), w_up.astype(BF16), w_down.astype(BF16)
    for l in range(DEPTH):
        lam_init = 0.8 - 0.6 * math.exp(-0.3 * l)
        z = _inproj(x2, w_in_p[l])
        aq, ak, av, cq, ck = _prep(z, tab_a, tab_c, seq)
        out_a = _diff_attn(aq, ak, av, diff_lambda[l], diff_subln_g[l], lam_init, batch, seq)
        out_b = _sgu(z, sgu_ln_g[l], sgu_ln_b[l], sgu_w[l], sgu_b[l])
        out_c = _dilated(cq, ck, z, batch, seq)
        out_d = _mlstm(z, mlstm_conv_w[l], mlstm_conv_b[l], mlstm_gate_b[l], mlstm_norm_g[l], batch, seq)
        x2 = _outproj(out_a, out_b, out_c, out_d, w_out_b[l], x2, ln1_g[l], ln1_b[l])
        x2 = _ffn(x2, w_gate_b[l], w_up_b[l], w_down_b[l], ln2_g[l], ln2_b[l])
    return x2.reshape(batch, seq, D_MODEL)
```

```python
import functools
import math

import jax
import jax.numpy as jnp
from jax import lax
from jax.experimental import pallas as pl
from jax.experimental.pallas import tpu as pltpu

F32 = jnp.float32
BF16 = jnp.bfloat16

D_MODEL = 1024
DEPTH = 4
N_HEADS = 4
HEAD_DIM = 64
GROUP = N_HEADS * HEAD_DIM
DIFF_QK = HEAD_DIM // 2
CHUNK = 128
DILATIONS = (1, 4, 16)
SPAN = CHUNK * DILATIONS[-1]
DIL_GROUP = 4
CONV_WIDTH = 4
ROPE_THETA = 500000.0
D_FF = 2816
FF_CHUNK = 256
IN_WIDTH = 3080
IN_PAD = 3200
COL_AQ, COL_AK, COL_AV, COL_CQ, COL_CK = 0, 256, 512, 1280, 1536
REST_CHUNKS = ((768, 0, 512), (1792, 512, 512), (2304, 1024, 512), (2816, 1536, 384))
REST_WIDTH = 1920
REST_BU, REST_BV, REST_CV, REST_DQ, REST_DK, REST_DV, REST_DO = range(7)
REST_GATES = 1792 // 128
ATT_TILE = 256
ACC_ROWS = HEAD_DIM + 16
LOG2E = math.log2(math.e)
ALPHA = (2 * DEPTH) ** 0.25
LN_EPS = 1e-5
NEG_INF = float("-inf")
VMEM_LIMIT = 56 * 1024 * 1024


def _cparams(sem):
    return pltpu.CompilerParams(dimension_semantics=sem, vmem_limit_bytes=VMEM_LIMIT)


def _head_select(head_of_lane, parts):
    out = parts[-1]
    for h in range(len(parts) - 2, -1, -1):
        out = jnp.where(head_of_lane == h, parts[h], out)
    return out


def _layer_norm_rows(y, g, b):
    mu = jnp.mean(y, axis=-1, keepdims=True)
    yc = y - mu
    var = jnp.mean(yc * yc, axis=-1, keepdims=True)
    return yc * lax.rsqrt(var + LN_EPS) * g + b


def _rope(x, c, s_lo, s_hi, half):
    width = x.shape[1]
    return x * c + pltpu.roll(x, width - half, 1) * s_lo + pltpu.roll(x, half, 1) * s_hi


def _inproj_kernel(x_ref, w_ref, ta_ref, tc_ref, rest_ref, oaq_ref, oak_ref, oav_ref, ocq_ref, ock_ref):
    xb = x_ref[...].astype(BF16)

    def project(col0, width=GROUP):
        return jnp.dot(xb, w_ref[:, col0:col0 + width], preferred_element_type=F32)

    ta_c, ta_lo, ta_hi = ta_ref[0], ta_ref[1], ta_ref[2]
    tc_c, tc_lo, tc_hi = tc_ref[0], tc_ref[1], tc_ref[2]
    half_a = DIFF_QK // 8
    half_c = HEAD_DIM // 8
    q = _rope(project(COL_AQ), ta_c, ta_lo, ta_hi, half_a) * (DIFF_QK ** -0.5 * LOG2E)
    oak_ref[...] = _rope(project(COL_AK), ta_c, ta_lo, ta_hi, half_a).astype(BF16)
    v = project(COL_AV)
    for t in range(q.shape[0] // ATT_TILE):
        rows = slice(t * ATT_TILE, (t + 1) * ATT_TILE)
        oaq_ref[t] = jnp.transpose(q[rows])
        oav_ref[t] = jnp.transpose(v[rows]).astype(BF16)
    ocq_ref[...] = _rope(project(COL_CQ), tc_c, tc_lo, tc_hi, half_c) * (HEAD_DIM ** -0.5 * LOG2E)
    ock_ref[...] = _rope(project(COL_CK), tc_c, tc_lo, tc_hi, half_c)
    for src, dst, width in REST_CHUNKS:
        rest_ref[:, dst:dst + width] = project(src, width)


def _inproj(x2, w, tab_a, tab_c, seq):
    m = x2.shape[0]
    tm = 512
    nt = seq // tm
    tspec = pl.BlockSpec((3, tm, GROUP), lambda i: (0, i % nt, 0))
    ospec = pl.BlockSpec((tm, GROUP), lambda i: (i, 0))
    tspec_out = pl.BlockSpec((tm // ATT_TILE, GROUP, ATT_TILE), lambda i: (i, 0, 0))
    transposed = lambda dt: jax.ShapeDtypeStruct((m // ATT_TILE, GROUP, ATT_TILE), dt)
    return pl.pallas_call(
        _inproj_kernel,
        grid=(m // tm,),
        in_specs=[pl.BlockSpec((tm, D_MODEL), lambda i: (i, 0)),
                  pl.BlockSpec((D_MODEL, IN_PAD), lambda i: (0, 0)), tspec, tspec],
        out_specs=[pl.BlockSpec((tm, REST_WIDTH), lambda i: (i, 0)), tspec_out, ospec, tspec_out, ospec, ospec],
        out_shape=[jax.ShapeDtypeStruct((m, REST_WIDTH), F32), transposed(F32),
                   jax.ShapeDtypeStruct((m, GROUP), BF16), transposed(BF16),
                   jax.ShapeDtypeStruct((m, GROUP), F32), jax.ShapeDtypeStruct((m, GROUP), F32)],
        compiler_params=_cparams(("parallel",)),
        name="inproj",
    )(x2, w, tab_a, tab_c)


def _rope_table(seq, group_dim, n_groups):
    rot = group_dim // 4
    half = rot // 2
    pos = jnp.arange(seq, dtype=F32)
    inv = ROPE_THETA ** (-jnp.arange(0, rot, 2, dtype=F32) / rot)
    ang = pos[:, None] * inv[None, :]
    cos, sin = jnp.cos(ang), jnp.sin(ang)
    ones = jnp.ones((seq, group_dim - rot), F32)
    zeros_half = jnp.zeros((seq, half), F32)
    zeros_rest = jnp.zeros((seq, group_dim - rot), F32)
    c = jnp.concatenate([cos, cos, ones], axis=1)
    lo = jnp.concatenate([-sin, zeros_half, zeros_rest], axis=1)
    hi = jnp.concatenate([zeros_half, sin, zeros_rest], axis=1)
    return jnp.stack([jnp.tile(t, (1, n_groups)) for t in (c, lo, hi)], axis=0)


def _diff_attn_kernel(lamv_ref, g_ref, qt_ref, k_ref, vt_ref, o_ref, qm_ref, m_ref, acc_ref, sa_ref, sb_ref,
                      *, lam_init):
    t = ATT_TILE
    i = pl.program_id(1)
    map_of_row = lax.broadcasted_iota(jnp.int32, (GROUP, 1), 0) // DIFF_QK
    qt = qt_ref[...]
    for hm in range(2 * N_HEADS):
        qm_ref[hm] = jnp.where(map_of_row == hm, qt, 0.0).astype(BF16)
    m_ref[...] = jnp.full(m_ref.shape, NEG_INF, F32)
    acc_ref[...] = jnp.zeros(acc_ref.shape, F32)
    ones = jnp.ones((ACC_ROWS - HEAD_DIM, t), BF16)

    def key_tile(j):
        return k_ref[pl.ds(pl.multiple_of(j * t, t), t), :]

    def consume(src_ref, j, hm, masked):
        s = src_ref[hm]
        if masked:
            key_pos = j * t + lax.broadcasted_iota(jnp.int32, (t, t), 0)
            q_pos = i * t + lax.broadcasted_iota(jnp.int32, (t, t), 1)
            s = jnp.where(key_pos <= q_pos, s, NEG_INF)
        m_prev = m_ref[hm:hm + 1, :]
        m_new = jnp.maximum(m_prev, jnp.max(s, axis=0, keepdims=True))
        alpha = jnp.exp2(m_prev - m_new)
        p = jnp.exp2(s - m_new).astype(BF16)
        h = hm // 2
        v_ext = jnp.concatenate([vt_ref[j, h * HEAD_DIM:(h + 1) * HEAD_DIM, :], ones], axis=0)
        acc_ref[hm] = alpha * acc_ref[hm] + jnp.dot(v_ext, p, preferred_element_type=F32)
        m_ref[hm:hm + 1, :] = m_new

    def stage(src_ref, j, dst_ref, masked):
        k_next = None if dst_ref is None else key_tile(j + 1)
        for hm in range(2 * N_HEADS):
            if dst_ref is not None:
                dst_ref[hm] = jnp.dot(k_next, qm_ref[hm], preferred_element_type=F32)
            consume(src_ref, j, hm, masked)

    k0 = key_tile(0)
    for hm in range(2 * N_HEADS):
        sa_ref[hm] = jnp.dot(k0, qm_ref[hm], preferred_element_type=F32)

    n_plain = i // 2

    def pair(first):
        stage(sa_ref, first, sb_ref, False)
        stage(sb_ref, first + 1, sa_ref, False)

    def two_pairs(u, carry):
        pair(4 * u)
        pair(4 * u + 2)
        return carry

    def one_pair(u, carry):
        pair(2 * u)
        return carry

    lax.fori_loop(0, n_plain // 2, two_pairs, 0)
    lax.fori_loop(2 * (n_plain // 2), n_plain, one_pair, 0)
    last = 2 * n_plain

    @pl.when(i % 2 == 1)
    def _():
        stage(sa_ref, last, sb_ref, True)
        stage(sb_ref, last + 1, None, True)

    @pl.when(i % 2 == 0)
    def _():
        stage(sa_ref, last, None, True)

    lv = lamv_ref[...]
    lam = (jnp.exp(jnp.sum(lv[0:1] * lv[1:2], axis=1, keepdims=True))
           - jnp.exp(jnp.sum(lv[2:3] * lv[3:4], axis=1, keepdims=True)) + lam_init)
    outs = []
    for h in range(N_HEADS):
        a0, a1 = acc_ref[2 * h], acc_ref[2 * h + 1]
        a = a0[:HEAD_DIM] / a0[HEAD_DIM:HEAD_DIM + 1] - lam * (a1[:HEAD_DIM] / a1[HEAD_DIM:HEAD_DIM + 1])
        ms = jnp.mean(a * a, axis=0, keepdims=True)
        outs.append(a * lax.rsqrt(ms + LN_EPS))
    o_ref[...] = (jnp.transpose(jnp.concatenate(outs, axis=0)) * g_ref[...] * (1.0 - lam_init)).astype(BF16)


def _diff_attn(aq_t, ak, av_t, lam_vecs, subln_g, lam_init, batch, seq):
    t = ATT_TILE
    nq = seq // t
    return pl.pallas_call(
        functools.partial(_diff_attn_kernel, lam_init=lam_init),
        grid=(batch, nq),
        in_specs=[pl.BlockSpec((4, DIFF_QK), lambda b, i: (0, 0)),
                  pl.BlockSpec((1, GROUP), lambda b, i: (0, 0)),
                  pl.BlockSpec((None, GROUP, t), lambda b, i: (b * nq + i, 0, 0)),
                  pl.BlockSpec((seq, GROUP), lambda b, i: (b, 0)),
                  pl.BlockSpec((nq, GROUP, t), lambda b, i: (b, 0, 0))],
        out_specs=pl.BlockSpec((t, GROUP), lambda b, i: (b * nq + i, 0)),
        out_shape=jax.ShapeDtypeStruct((batch * seq, GROUP), BF16),
        scratch_shapes=[pltpu.VMEM((2 * N_HEADS, GROUP, t), BF16),
                        pltpu.VMEM((2 * N_HEADS, t), F32),
                        pltpu.VMEM((2 * N_HEADS, ACC_ROWS, t), F32),
                        pltpu.VMEM((2 * N_HEADS, t, t), F32),
                        pltpu.VMEM((2 * N_HEADS, t, t), F32)],
        compiler_params=_cparams(("parallel", "arbitrary")),
        name="diff_attn",
    )(lam_vecs, jnp.tile(subln_g, N_HEADS)[None, :], aq_t, ak, av_t)


def _sgu_kernel(u_ref, v_ref, g_ref, b_ref, w_ref, bt_ref, o_ref, *, tm):
    lane = lax.broadcasted_iota(jnp.int32, (1, GROUP), 1)
    head_of_lane = lane // HEAD_DIM
    u = jax.nn.gelu(u_ref[...])
    vn = _layer_norm_rows(jax.nn.gelu(v_ref[...]), g_ref[...], b_ref[...]).astype(BF16)
    r = lax.broadcasted_iota(jnp.int32, (CHUNK, CHUNK), 0)
    c = lax.broadcasted_iota(jnp.int32, (CHUNK, CHUNK), 1)
    w_all = jnp.concatenate([jnp.where(c <= r, w_ref[g], 0.0) for g in range(N_HEADS)], axis=0).astype(BF16)
    for ch in range(tm // CHUNK):
        rows = slice(ch * CHUNK, (ch + 1) * CHUNK)
        res = jnp.dot(w_all, vn[rows], preferred_element_type=F32)
        zc = _head_select(head_of_lane, [res[g * CHUNK:(g + 1) * CHUNK] for g in range(N_HEADS)])
        o_ref[rows, :] = (u[rows] * (zc + bt_ref[...])).astype(BF16)


def _sgu(z, ln_g, ln_b, w_s, b_s):
    m = z.shape[0]
    tm = 512
    bias_tab = jnp.repeat(jnp.transpose(b_s), HEAD_DIM, axis=1)
    return pl.pallas_call(
        functools.partial(_sgu_kernel, tm=tm),
        grid=(m // tm,),
        in_specs=[pl.BlockSpec((tm, GROUP), lambda i: (i, REST_BU)),
                  pl.BlockSpec((tm, GROUP), lambda i: (i, REST_BV)),
                  pl.BlockSpec((1, GROUP), lambda i: (0, 0)),
                  pl.BlockSpec((1, GROUP), lambda i: (0, 0)),
                  pl.BlockSpec((N_HEADS, CHUNK, CHUNK), lambda i: (0, 0, 0)),
                  pl.BlockSpec((CHUNK, GROUP), lambda i: (0, 0))],
        out_specs=pl.BlockSpec((tm, GROUP), lambda i: (i, 0)),
        out_shape=jax.ShapeDtypeStruct((m, GROUP), BF16),
        compiler_params=_cparams(("parallel",)),
        name="sgu",
    )(z, z, ln_g[None, :], ln_b[None, :], w_s, bias_tab)


def _dilated_kernel(*refs):
    q_ref, kc_ref, kp_ref, vc_ref, vp_ref = [refs[2 * t:2 * t + 2] for t in range(5)]
    o_ref = refs[10:12]
    os_ref = [refs[12 + 2 * p:14 + 2 * p] for p in range(3)]
    ls_ref = [refs[18 + 2 * p:20 + 2 * p] for p in range(3)]
    n = pl.program_id(1)
    lane = lax.broadcasted_iota(jnp.int32, (1, GROUP), 1)
    head_of_lane = lane // HEAD_DIM
    i_idx = lax.broadcasted_iota(jnp.int32, (CHUNK, 2 * CHUNK), 0)
    j_idx = lax.broadcasted_iota(jnp.int32, (CHUNK, 2 * CHUNK), 1)
    dist = CHUNK + i_idx - j_idx
    band_bias = jnp.where((dist >= 0) & (dist <= CHUNK), 0.0, NEG_INF)
    prev_half_bias = jnp.where(j_idx >= CHUNK, 0.0, NEG_INF)

    def rows(start, d):
        return pl.ds(start, CHUNK) if d == 1 else pl.ds(start, CHUNK, stride=d)

    def strided(pair, start, d):
        return jnp.concatenate([pair[0][rows(start, d), :], pair[1][rows(start, d), :]], axis=1)

    def store(pair, start, d, val):
        pair[0][rows(start, d), :] = val[:, :128]
        pair[1][rows(start, d), :] = val[:, 128:]

    def scores(d, idx):
        nblk = SPAN // (CHUNK * d)
        r, nb = idx // nblk, idx % nblk
        start = nb * (CHUNK * d) + r
        first = nb == 0
        in_span = jnp.maximum(start - CHUNK * d, 0)
        in_prev = (nblk - 1) * (CHUNK * d) + r
        k_prev = jnp.where(first, strided(kp_ref, in_prev, d), strided(kc_ref, in_span, d))
        v_prev = jnp.where(first, strided(vp_ref, in_prev, d), strided(vc_ref, in_span, d))
        qb = strided(q_ref, start, d).astype(BF16)
        kk = jnp.concatenate([k_prev, strided(kc_ref, start, d)], axis=0).astype(BF16)
        vv = jnp.concatenate([v_prev, strided(vc_ref, start, d)], axis=0).astype(BF16)
        q_zero = jnp.zeros_like(qb)
        q_heads = jnp.concatenate([jnp.where(head_of_lane == h, qb, q_zero) for h in range(N_HEADS)], axis=0)
        s_all = lax.dot_general(q_heads, kk, (((1,), (1,)), ((), ())), preferred_element_type=F32)
        ss = [s_all[h * CHUNK:(h + 1) * CHUNK] for h in range(N_HEADS)]
        bias = band_bias + jnp.where(first & (n == 0), prev_half_bias, 0.0)
        return ss, vv, bias, start

    def finish(pi, d, ss, vv, bias, start):
        probs, lses = [], []
        for h in range(N_HEADS):
            s = ss[h] + bias
            mx = jnp.max(s, axis=1, keepdims=True)
            p = jnp.exp2(s - mx)
            den = jnp.sum(p, axis=1, keepdims=True)
            probs.append((p / den).astype(BF16))
            lses.append(mx + jnp.log2(den))
        v_zero = jnp.zeros_like(vv)
        v_bd = jnp.concatenate([jnp.where(head_of_lane == h, vv, v_zero) for h in range(N_HEADS)], axis=0)
        ob = jnp.dot(jnp.concatenate(probs, axis=1), v_bd, preferred_element_type=F32)
        lb = _head_select(head_of_lane, lses) + jnp.zeros((CHUNK, GROUP), F32)
        store(os_ref[pi], start, d, ob)
        store(ls_ref[pi], start, d, lb)

    blocks_per_pattern = SPAN // CHUNK
    for pi, d in enumerate(DILATIONS):
        def group(t, carry, pi=pi, d=d):
            trips = blocks_per_pattern // DIL_GROUP
            staged = [scores(d, t + g * trips) for g in range(DIL_GROUP)]
            for item in staged:
                finish(pi, d, *item)
            return carry

        lax.fori_loop(0, blocks_per_pattern // DIL_GROUP, group, 0)

    for half in range(2):
        l0, l1, l2 = ls_ref[0][half][...], ls_ref[1][half][...], ls_ref[2][half][...]
        mx = jnp.maximum(jnp.maximum(l0, l1), l2)
        w0, w1, w2 = jnp.exp2(l0 - mx), jnp.exp2(l1 - mx), jnp.exp2(l2 - mx)
        tot = w0 + w1 + w2
        o_ref[half][...] = ((w0 / tot) * os_ref[0][half][...] + (w1 / tot) * os_ref[1][half][...]
                            + (w2 / tot) * os_ref[2][half][...]).astype(BF16)


def _dilated(cq, ck, z, batch, seq):
    ns = seq // SPAN
    blk = (SPAN, 128)

    def halves(col, prev):
        def spec(c):
            if prev:
                return pl.BlockSpec(blk, lambda b, n: (b * ns + jnp.maximum(n - 1, 0), c))
            return pl.BlockSpec(blk, lambda b, n: (b * ns + n, c))
        return [spec(2 * col), spec(2 * col + 1)]

    v_col = REST_CV
    return pl.pallas_call(
        _dilated_kernel,
        grid=(batch, ns),
        in_specs=(halves(0, False) + halves(0, False) + halves(0, True)
                  + halves(v_col, False) + halves(v_col, True)),
        out_specs=[pl.BlockSpec(blk, lambda b, n: (b * ns + n, 0))] * 2,
        out_shape=[jax.ShapeDtypeStruct((batch * seq, 128), BF16)] * 2,
        scratch_shapes=[pltpu.VMEM(blk, F32)] * 12,
        compiler_params=_cparams(("parallel", "arbitrary")),
        name="dilated",
    )(cq, cq, ck, ck, ck, ck, z, z, z, z)


def _lane_pick(x, lane, idx):
    return jnp.sum(jnp.where(lane == idx, x, 0.0), axis=1, keepdims=True)


def _mlstm_kernel(*refs, n_seq):
    ins = [refs[5 * p:5 * p + 5] for p in range(n_seq)]
    cw_ref, cb_ref, gb_ref, ng_ref = refs[5 * n_seq:5 * n_seq + 4]
    o_ref = refs[5 * n_seq + 4]
    states = [refs[5 * n_seq + 5 + 4 * p:5 * n_seq + 9 + 4 * p] for p in range(n_seq)]
    c_idx = pl.program_id(0)
    L = CHUNK
    lane = lax.broadcasted_iota(jnp.int32, (1, GROUP), 1)
    head_of_lane = lane // HEAD_DIM
    lane1 = lax.broadcasted_iota(jnp.int32, (1, 128), 1)
    row_head = lax.broadcasted_iota(jnp.int32, (GROUP, 1), 0) // HEAD_DIM
    rows = lax.broadcasted_iota(jnp.int32, (L, 1), 0)
    tril = lax.broadcasted_iota(jnp.int32, (L, L), 1) <= lax.broadcasted_iota(jnp.int32, (L, L), 0)
    tril_ones = jnp.where(tril, 1.0, 0.0)

    @pl.when(c_idx == 0)
    def _():
        for state in states:
            for ref in state:
                ref[...] = jnp.zeros(ref.shape, F32)

    def sequence(p):
        q_ref, k_ref, v_ref, og_ref, gt_ref = ins[p]
        tail_ref, ct_ref, nm_ref, ms_ref = states[p]

        gates = gt_ref[...] + gb_ref[...]
        log_f = jnp.minimum(gates, 0.0) - jnp.log1p(jnp.exp(-jnp.abs(gates)))
        xg = jnp.where(lane1 < N_HEADS, gates, jnp.where(lane1 < 2 * N_HEADS, log_f, 0.0))
        csum = jnp.dot(tril_ones, xg, preferred_element_type=F32,
                       precision=lax.Precision.HIGHEST)

        x = jnp.concatenate([q_ref[...], k_ref[...]], axis=1)
        tail = tail_ref[...]
        pad = jnp.zeros((L - 8, 2 * GROUP), F32)
        y = x * cw_ref[CONV_WIDTH - 1:CONV_WIDTH, :]
        for back in range(1, CONV_WIDTH):
            shifted = jnp.where(rows >= back, pltpu.roll(x, back, 0),
                                jnp.concatenate([pltpu.roll(tail, back, 0), pad], axis=0))
            y = y + shifted * cw_ref[CONV_WIDTH - 1 - back:CONV_WIDTH - back, :]
        tail_ref[...] = x[L - 8:, :]
        qk = jax.nn.silu(y + cb_ref[...])
        mq = qk[:, :GROUP].astype(BF16)
        mk = (qk[:, GROUP:] * (HEAD_DIM ** -0.5)).astype(BF16)
        v = v_ref[...]
        vb = v.astype(BF16)
        yield

        q_zero = jnp.zeros_like(mq)
        q_heads = jnp.concatenate([jnp.where(head_of_lane == h, mq, q_zero) for h in range(N_HEADS)], axis=0)
        qk_all = lax.dot_general(q_heads, mk, (((1,), (1,)), ((), ())), preferred_element_type=F32)
        inter_num = jnp.dot(mq, ct_ref[...].astype(BF16), preferred_element_type=F32)
        nq = jnp.dot(mq, nm_ref[...].astype(BF16), preferred_element_type=F32)
        yield

        xg_t = jnp.transpose(xg)
        csum_t = jnp.transpose(csum)
        m_state = ms_ref[...]
        sws, e_cols, mout_cols, den_cols, w_cols = [], [], [], [], []
        a_sc, e_sc, m_news = [], [], []
        for h in range(N_HEADS):
            b_c = _lane_pick(csum, lane1, N_HEADS + h)
            i_c = _lane_pick(xg, lane1, h)
            b_r = csum_t[N_HEADS + h:N_HEADS + h + 1, :]
            i_r = xg_t[h:h + 1, :]
            b_last = _lane_pick(b_r, lane1, L - 1)
            m_prev = _lane_pick(m_state, lane1, h)
            d_log = jnp.where(tril, b_c - b_r + i_r, NEG_INF)
            g_c = b_last - b_c + i_c
            g_max = jnp.max(g_c, axis=0, keepdims=True)
            w_cols.append(jnp.exp(g_c - g_max))
            inter = b_c + m_prev
            m_out = jnp.maximum(inter, jnp.max(d_log, axis=1, keepdims=True))
            p_mat = jnp.exp(d_log - m_out)
            sw = p_mat * qk_all[h * L:(h + 1) * L]
            e_t = jnp.exp(inter - m_out)
            den_cols.append(jnp.sum(sw, axis=1, keepdims=True) + e_t * _lane_pick(nq, lane1, h))
            sws.append(sw.astype(BF16))
            e_cols.append(e_t)
            mout_cols.append(m_out)
            m_new = jnp.maximum(b_last + m_prev, g_max)
            a_sc.append(jnp.exp(b_last + m_prev - m_new))
            e_sc.append(jnp.exp(g_max - m_new))
            m_news.append(m_new)
        yield

        v_zero = jnp.zeros_like(vb)
        v_bd = jnp.concatenate([jnp.where(head_of_lane == h, vb, v_zero) for h in range(N_HEADS)], axis=0)
        num = (jnp.dot(jnp.concatenate(sws, axis=1), v_bd, preferred_element_type=F32)
               + _head_select(head_of_lane, e_cols) * inter_num)
        wv = (_head_select(head_of_lane, w_cols) * v).astype(BF16)
        ct_loc = lax.dot_general(mk, wv, (((0,), (0,)), ((), ())), preferred_element_type=F32)
        w_mat = jnp.zeros((L, 128), F32)
        for h in range(N_HEADS):
            w_mat = jnp.where(lane1 == h, w_cols[h], w_mat)
        nm_loc = lax.dot_general(mk, w_mat.astype(BF16), (((0,), (0,)), ((), ())), preferred_element_type=F32)
        yield

        den = _head_select(head_of_lane, den_cols)
        hid = num / jnp.maximum(jnp.abs(den), jnp.exp(-_head_select(head_of_lane, mout_cols)))
        mus = [jnp.sum(jnp.where(head_of_lane == h, hid, 0.0), axis=1, keepdims=True) * (1.0 / HEAD_DIM)
               for h in range(N_HEADS)]
        hc = hid - _head_select(head_of_lane, mus)
        hc2 = hc * hc
        var = [jnp.sum(jnp.where(head_of_lane == h, hc2, 0.0), axis=1, keepdims=True) * (1.0 / HEAD_DIM)
               for h in range(N_HEADS)]
        normed = hc * lax.rsqrt(_head_select(head_of_lane, var) + LN_EPS) * ng_ref[...]
        o_ref[p] = (jax.nn.sigmoid(og_ref[...]) * normed).astype(BF16)

        ct_loc = jnp.where(row_head == head_of_lane, ct_loc, 0.0)
        ct_ref[...] = _head_select(row_head, a_sc) * ct_ref[...] + _head_select(row_head, e_sc) * ct_loc
        nm_loc = jnp.where(row_head == lane1, nm_loc, 0.0)
        nm_ref[...] = _head_select(row_head, a_sc) * nm_ref[...] + _head_select(row_head, e_sc) * nm_loc
        m_row = jnp.zeros((1, 128), F32)
        for h in range(N_HEADS):
            m_row = jnp.where(lane1 == h, m_news[h], m_row)
        ms_ref[...] = m_row
        yield

    chains = [sequence(p) for p in range(n_seq)]
    for _ in range(5):
        for chain in chains:
            next(chain)


def _mlstm(z, conv_w, conv_b, gate_b, norm_g, batch, seq):
    nc = seq // CHUNK
    const = lambda c: (0, 0)
    gate_tab = jnp.zeros((1, 128), F32).at[0, :2 * N_HEADS].set(gate_b.reshape(-1))
    seq_specs = []
    for b in range(batch):
        row = lambda col, b=b: (lambda c: (b * nc + c, col))
        seq_specs += [pl.BlockSpec((CHUNK, GROUP), row(REST_DQ)), pl.BlockSpec((CHUNK, GROUP), row(REST_DK)),
                      pl.BlockSpec((CHUNK, GROUP), row(REST_DV)), pl.BlockSpec((CHUNK, GROUP), row(REST_DO)),
                      pl.BlockSpec((CHUNK, 128), row(REST_GATES))]
    state = [pltpu.VMEM((8, 2 * GROUP), F32), pltpu.VMEM((GROUP, GROUP), F32),
             pltpu.VMEM((GROUP, 128), F32), pltpu.VMEM((1, 128), F32)]
    return pl.pallas_call(
        functools.partial(_mlstm_kernel, n_seq=batch),
        grid=(nc,),
        in_specs=seq_specs + [pl.BlockSpec((CONV_WIDTH, 2 * GROUP), const), pl.BlockSpec((1, 2 * GROUP), const),
                              pl.BlockSpec((1, 128), const), pl.BlockSpec((1, GROUP), const)],
        out_specs=pl.BlockSpec((batch, CHUNK, GROUP), lambda c: (0, c, 0)),
        out_shape=jax.ShapeDtypeStruct((batch, seq, GROUP), BF16),
        scratch_shapes=state * batch,
        compiler_params=_cparams(("arbitrary",)),
        name="mlstm",
    )(*([z] * (5 * batch)), conv_w, conv_b[None, :], gate_tab,
      jnp.tile(norm_g, N_HEADS)[None, :]).reshape(batch * seq, GROUP)


def _outproj_kernel(a_ref, b_ref, c_lo_ref, c_hi_ref, d_ref, w_ref, x_ref, g_ref, bb_ref, o_ref):
    mixed = jnp.concatenate([a_ref[...], b_ref[...], c_lo_ref[...], c_hi_ref[...], d_ref[...]], axis=1)
    h = jnp.dot(mixed, w_ref[...], preferred_element_type=F32)
    o_ref[...] = _layer_norm_rows(ALPHA * x_ref[...] + h, g_ref[...], bb_ref[...])


def _outproj(oa, ob, oc, od, w, x2, g, b):
    m = x2.shape[0]
    tm = 512
    part = pl.BlockSpec((tm, GROUP), lambda i: (i, 0))
    half = pl.BlockSpec((tm, 128), lambda i: (i, 0))
    full = pl.BlockSpec((tm, D_MODEL), lambda i: (i, 0))
    vec = pl.BlockSpec((1, D_MODEL), lambda i: (0, 0))
    return pl.pallas_call(
        _outproj_kernel,
        grid=(m // tm,),
        in_specs=[part, part, half, half, part, pl.BlockSpec((D_MODEL, D_MODEL), lambda i: (0, 0)),
                  full, vec, vec],
        out_specs=full,
        out_shape=jax.ShapeDtypeStruct((m, D_MODEL), F32),
        compiler_params=_cparams(("parallel",)),
        name="outproj",
    )(oa, ob, oc[0], oc[1], od, w, x2, g[None, :], b[None, :])


def _ffn_kernel(x_ref, wg_ref, wu_ref, wd_ref, g_ref, b_ref, o_ref):
    x = x_ref[...]
    xb = x.astype(BF16)
    hidden = []
    for c in range(D_FF // FF_CHUNK):
        cols = slice(c * FF_CHUNK, (c + 1) * FF_CHUNK)
        gate = jnp.dot(xb, wg_ref[:, cols], preferred_element_type=F32)
        up = jnp.dot(xb, wu_ref[:, cols], preferred_element_type=F32)
        hidden.append((jax.nn.silu(gate) * up).astype(BF16))
    f = jnp.dot(jnp.concatenate(hidden, axis=1), wd_ref[...], preferred_element_type=F32)
    o_ref[...] = _layer_norm_rows(ALPHA * x + f, g_ref[...], b_ref[...])


def _ffn(x2, wg, wu, wd, g, b):
    m = x2.shape[0]
    tm = 512
    full = pl.BlockSpec((tm, D_MODEL), lambda i: (i, 0))
    vec = pl.BlockSpec((1, D_MODEL), lambda i: (0, 0))
    return pl.pallas_call(
        _ffn_kernel,
        grid=(m // tm,),
        in_specs=[full, pl.BlockSpec((D_MODEL, D_FF), lambda i: (0, 0)),
                  pl.BlockSpec((D_MODEL, D_FF), lambda i: (0, 0)),
                  pl.BlockSpec((D_FF, D_MODEL), lambda i: (0, 0)), vec, vec],
        out_specs=full,
        out_shape=jax.ShapeDtypeStruct((m, D_MODEL), F32),
        compiler_params=_cparams(("parallel",)),
        name="ffn",
    )(x2, wg, wu, wd, g[None, :], b[None, :])


def kernel(x, w_in, diff_lambda, diff_subln_g, sgu_ln_g, sgu_ln_b, sgu_w, sgu_b, mlstm_conv_w, mlstm_conv_b,
           mlstm_gate_b, mlstm_norm_g, w_out, ln1_g, ln1_b, w_gate, w_up, w_down, ln2_g, ln2_b):
    batch, seq, _ = x.shape
    assert seq % SPAN == 0
    x2 = x.reshape(batch * seq, D_MODEL)
    tab_a = _rope_table(seq, DIFF_QK, GROUP // DIFF_QK)
    tab_c = _rope_table(seq, HEAD_DIM, N_HEADS)
    w_in_p = jnp.pad(w_in, ((0, 0), (0, 0), (0, IN_PAD - IN_WIDTH))).astype(BF16)
    w_out_b = w_out.astype(BF16)
    w_gate_b, w_up_b, w_down_b = w_gate.astype(BF16), w_up.astype(BF16), w_down.astype(BF16)
    for l in range(DEPTH):
        lam_init = 0.8 - 0.6 * math.exp(-0.3 * l)
        z, aq, ak, av, cq, ck = _inproj(x2, w_in_p[l], tab_a, tab_c, seq)
        out_a = _diff_attn(aq, ak, av, diff_lambda[l], diff_subln_g[l], lam_init, batch, seq)
        out_b = _sgu(z, sgu_ln_g[l], sgu_ln_b[l], sgu_w[l], sgu_b[l])
        out_c = _dilated(cq, ck, z, batch, seq)
        out_d = _mlstm(z, mlstm_conv_w[l], mlstm_conv_b[l], mlstm_gate_b[l], mlstm_norm_g[l], batch, seq)
        x2 = _outproj(out_a, out_b, out_c, out_d, w_out_b[l], x2, ln1_g[l], ln1_b[l])
        x2 = _ffn(x2, w_gate_b[l], w_up_b[l], w_down_b[l], ln2_g[l], ln2_b[l])
    return x2.reshape(batch, seq, D_MODEL)
```

```python
import functools
import math

import jax
import jax.numpy as jnp
from jax import lax
from jax.experimental import pallas as pl
from jax.experimental.pallas import tpu as pltpu

F32 = jnp.float32
BF16 = jnp.bfloat16

D_MODEL = 1024
DEPTH = 4
N_HEADS = 4
HEAD_DIM = 64
GROUP = N_HEADS * HEAD_DIM
DIFF_QK = HEAD_DIM // 2
CHUNK = 128
DILATIONS = (1, 4, 16)
SPAN = CHUNK * DILATIONS[-1]
DIL_GROUP = 4
CONV_WIDTH = 4
ROPE_THETA = 500000.0
D_FF = 2816
FF_CHUNK = 256
IN_WIDTH = 3080
IN_PAD = 3200
COL_AQ, COL_AK, COL_AV, COL_CQ, COL_CK = 0, 256, 512, 1280, 1536
REST_CHUNKS = ((768, 0, 512), (1792, 512, 512), (2304, 1024, 512), (2816, 1536, 384))
REST_WIDTH = 1920
REST_BU, REST_BV, REST_CV, REST_DQ, REST_DK, REST_DV, REST_DO = range(7)
REST_GATES = 1792 // 128
ATT_TILE = 256
ATT_TQ = 2 * ATT_TILE
ACC_ROWS = HEAD_DIM + 16
LOG2E = math.log2(math.e)
ALPHA = (2 * DEPTH) ** 0.25
LN_EPS = 1e-5
NEG_INF = float("-inf")
VMEM_LIMIT = 56 * 1024 * 1024


def _cparams(sem):
    return pltpu.CompilerParams(dimension_semantics=sem, vmem_limit_bytes=VMEM_LIMIT)


def _head_select(head_of_lane, parts):
    out = parts[-1]
    for h in range(len(parts) - 2, -1, -1):
        out = jnp.where(head_of_lane == h, parts[h], out)
    return out


def _layer_norm_rows(y, g, b):
    mu = jnp.mean(y, axis=-1, keepdims=True)
    yc = y - mu
    var = jnp.mean(yc * yc, axis=-1, keepdims=True)
    return yc * lax.rsqrt(var + LN_EPS) * g + b


def _rope(x, c, s_lo, s_hi, half):
    width = x.shape[1]
    return x * c + pltpu.roll(x, width - half, 1) * s_lo + pltpu.roll(x, half, 1) * s_hi


def _inproj_kernel(x_ref, w_ref, ta_ref, tc_ref, rest_ref, oaq_ref, oak_ref, oav_ref, ocq_ref, ock_ref):
    xb = x_ref[...].astype(BF16)

    def project(col0, width=GROUP):
        return jnp.dot(xb, w_ref[:, col0:col0 + width], preferred_element_type=F32)

    ta_c, ta_lo, ta_hi = [jnp.concatenate([ta_ref[t]] * (GROUP // 128), axis=1) for t in range(3)]
    tc_c, tc_lo, tc_hi = [jnp.concatenate([tc_ref[t]] * (GROUP // 128), axis=1) for t in range(3)]
    half_a = DIFF_QK // 8
    half_c = HEAD_DIM // 8
    q = _rope(project(COL_AQ), ta_c, ta_lo, ta_hi, half_a) * (DIFF_QK ** -0.5 * LOG2E)
    oak_ref[...] = _rope(project(COL_AK), ta_c, ta_lo, ta_hi, half_a).astype(BF16)
    v = project(COL_AV)
    for t in range(q.shape[0] // ATT_TILE):
        rows = slice(t * ATT_TILE, (t + 1) * ATT_TILE)
        oaq_ref[t] = jnp.transpose(q[rows])
        oav_ref[t] = jnp.transpose(v[rows]).astype(BF16)
    ocq_ref[...] = _rope(project(COL_CQ), tc_c, tc_lo, tc_hi, half_c) * (HEAD_DIM ** -0.5 * LOG2E)
    ock_ref[...] = _rope(project(COL_CK), tc_c, tc_lo, tc_hi, half_c)
    for src, dst, width in REST_CHUNKS:
        rest_ref[:, dst:dst + width] = project(src, width)


def _inproj(x2, w, tab_a, tab_c, seq):
    m = x2.shape[0]
    tm = 512
    nt = seq // tm
    tspec = pl.BlockSpec((3, tm, 128), lambda i: (0, i % nt, 0))
    ospec = pl.BlockSpec((tm, GROUP), lambda i: (i, 0))
    tspec_out = pl.BlockSpec((tm // ATT_TILE, GROUP, ATT_TILE), lambda i: (i, 0, 0))
    transposed = lambda dt: jax.ShapeDtypeStruct((m // ATT_TILE, GROUP, ATT_TILE), dt)
    return pl.pallas_call(
        _inproj_kernel,
        grid=(m // tm,),
        in_specs=[pl.BlockSpec((tm, D_MODEL), lambda i: (i, 0)),
                  pl.BlockSpec((D_MODEL, IN_PAD), lambda i: (0, 0)), tspec, tspec],
        out_specs=[pl.BlockSpec((tm, REST_WIDTH), lambda i: (i, 0)), tspec_out, ospec, tspec_out, ospec, ospec],
        out_shape=[jax.ShapeDtypeStruct((m, REST_WIDTH), F32), transposed(F32),
                   jax.ShapeDtypeStruct((m, GROUP), BF16), transposed(BF16),
                   jax.ShapeDtypeStruct((m, GROUP), F32), jax.ShapeDtypeStruct((m, GROUP), F32)],
        compiler_params=_cparams(("parallel",)),
        name="inproj",
    )(x2, w, tab_a, tab_c)


def _rope_table(seq, group_dim, n_groups):
    rot = group_dim // 4
    half = rot // 2
    pos = jnp.arange(seq, dtype=F32)
    inv = ROPE_THETA ** (-jnp.arange(0, rot, 2, dtype=F32) / rot)
    ang = pos[:, None] * inv[None, :]
    cos, sin = jnp.cos(ang), jnp.sin(ang)
    ones = jnp.ones((seq, group_dim - rot), F32)
    zeros_half = jnp.zeros((seq, half), F32)
    zeros_rest = jnp.zeros((seq, group_dim - rot), F32)
    c = jnp.concatenate([cos, cos, ones], axis=1)
    lo = jnp.concatenate([-sin, zeros_half, zeros_rest], axis=1)
    hi = jnp.concatenate([zeros_half, sin, zeros_rest], axis=1)
    return jnp.stack([jnp.tile(t, (1, n_groups)) for t in (c, lo, hi)], axis=0)


def _diff_attn_kernel(lamv_ref, g_ref, qt_ref, k_ref, vt_ref, o_ref, qm_ref, m_ref, acc_ref,
                      sa_ref, sb_ref, xa_ref, xb_ref, *, lam_init):
    tq, tk = ATT_TQ, ATT_TILE
    i = pl.program_id(1)
    map_of_row = lax.broadcasted_iota(jnp.int32, (GROUP, 1), 0) // DIFF_QK
    qt = jnp.concatenate([qt_ref[c] for c in range(tq // ATT_TILE)], axis=1)
    for hm in range(2 * N_HEADS):
        qm_ref[hm] = jnp.where(map_of_row == hm, qt, 0.0).astype(BF16)
    m_ref[...] = jnp.full(m_ref.shape, NEG_INF, F32)
    acc_ref[...] = jnp.zeros(acc_ref.shape, F32)
    ones = jnp.ones((ACC_ROWS - HEAD_DIM, tk), BF16)

    def key_tile(j):
        return k_ref[pl.ds(pl.multiple_of(j * tk, tk), tk), :]

    def produce(dst, j, hm):
        scores = jnp.dot(key_tile(j), qm_ref[hm], preferred_element_type=F32)
        dst[0][hm] = scores
        dst[1][hm:hm + 1, :] = jnp.max(scores, axis=0, keepdims=True)

    def consume(src, j, hm, masked):
        s = src[0][hm]
        if masked:
            key_pos = j * tk + lax.broadcasted_iota(jnp.int32, (tk, tq), 0)
            q_pos = i * tq + lax.broadcasted_iota(jnp.int32, (tk, tq), 1)
            s = jnp.where(key_pos <= q_pos, s, NEG_INF)
            m_tile = jnp.max(s, axis=0, keepdims=True)
        else:
            m_tile = src[1][hm:hm + 1, :]
        m_prev = m_ref[hm:hm + 1, :]
        m_new = jnp.maximum(m_prev, m_tile)
        alpha = jnp.exp2(m_prev - m_new)
        p = jnp.exp2(s - m_new).astype(BF16)
        h = hm // 2
        v_ext = jnp.concatenate([vt_ref[j, h * HEAD_DIM:(h + 1) * HEAD_DIM, :], ones], axis=0)
        acc_ref[hm] = alpha * acc_ref[hm] + jnp.dot(v_ext, p, preferred_element_type=F32)
        m_ref[hm:hm + 1, :] = m_new

    def stage(src, j, dst, masked):
        for hm in range(2 * N_HEADS):
            if dst is not None:
                produce(dst, j + 1, hm)
            consume(src, j, hm, masked)

    buf_a, buf_b = (sa_ref, xa_ref), (sb_ref, xb_ref)
    for hm in range(2 * N_HEADS):
        produce(buf_a, 0, hm)

    n_plain = i

    def pair(first):
        stage(buf_a, first, buf_b, False)
        stage(buf_b, first + 1, buf_a, False)

    def two_pairs(u, carry):
        pair(4 * u)
        pair(4 * u + 2)
        return carry

    def one_pair(u, carry):
        pair(2 * u)
        return carry

    lax.fori_loop(0, n_plain // 2, two_pairs, 0)
    lax.fori_loop(2 * (n_plain // 2), n_plain, one_pair, 0)
    last = 2 * n_plain
    stage(buf_a, last, buf_b, True)
    stage(buf_b, last + 1, None, True)

    lv = lamv_ref[...]
    lam = (jnp.exp(jnp.sum(lv[0:1] * lv[1:2], axis=1, keepdims=True))
           - jnp.exp(jnp.sum(lv[2:3] * lv[3:4], axis=1, keepdims=True)) + lam_init)
    outs = []
    for h in range(N_HEADS):
        a0, a1 = acc_ref[2 * h], acc_ref[2 * h + 1]
        a = a0[:HEAD_DIM] / a0[HEAD_DIM:HEAD_DIM + 1] - lam * (a1[:HEAD_DIM] / a1[HEAD_DIM:HEAD_DIM + 1])
        ms = jnp.mean(a * a, axis=0, keepdims=True)
        outs.append(a * lax.rsqrt(ms + LN_EPS))
    o_ref[...] = (jnp.transpose(jnp.concatenate(outs, axis=0)) * g_ref[...] * (1.0 - lam_init)).astype(BF16)


def _diff_attn(aq_t, ak, av_t, lam_vecs, subln_g, lam_init, batch, seq):
    tq, tk = ATT_TQ, ATT_TILE
    nq = seq // tq
    maps = 2 * N_HEADS
    return pl.pallas_call(
        functools.partial(_diff_attn_kernel, lam_init=lam_init),
        grid=(batch, nq),
        in_specs=[pl.BlockSpec((4, DIFF_QK), lambda b, i: (0, 0)),
                  pl.BlockSpec((1, GROUP), lambda b, i: (0, 0)),
                  pl.BlockSpec((tq // tk, GROUP, tk), lambda b, i: (b * nq + i, 0, 0)),
                  pl.BlockSpec((seq, GROUP), lambda b, i: (b, 0)),
                  pl.BlockSpec((seq // tk, GROUP, tk), lambda b, i: (b, 0, 0))],
        out_specs=pl.BlockSpec((tq, GROUP), lambda b, i: (b * nq + i, 0)),
        out_shape=jax.ShapeDtypeStruct((batch * seq, GROUP), BF16),
        scratch_shapes=[pltpu.VMEM((maps, GROUP, tq), BF16),
                        pltpu.VMEM((maps, tq), F32),
                        pltpu.VMEM((maps, ACC_ROWS, tq), F32),
                        pltpu.VMEM((maps, tk, tq), F32), pltpu.VMEM((maps, tk, tq), F32),
                        pltpu.VMEM((maps, tq), F32), pltpu.VMEM((maps, tq), F32)],
        compiler_params=_cparams(("parallel", "arbitrary")),
        name="diff_attn",
    )(lam_vecs, jnp.tile(subln_g, N_HEADS)[None, :], aq_t, ak, av_t)


def _sgu_kernel(u_ref, v_ref, g_ref, b_ref, w_ref, bt_ref, o_ref, *, tm):
    lane = lax.broadcasted_iota(jnp.int32, (1, GROUP), 1)
    head_of_lane = lane // HEAD_DIM
    u = jax.nn.gelu(u_ref[...])
    vn = _layer_norm_rows(jax.nn.gelu(v_ref[...]), g_ref[...], b_ref[...]).astype(BF16)
    r = lax.broadcasted_iota(jnp.int32, (CHUNK, CHUNK), 0)
    c = lax.broadcasted_iota(jnp.int32, (CHUNK, CHUNK), 1)
    w_all = jnp.concatenate([jnp.where(c <= r, w_ref[g], 0.0) for g in range(N_HEADS)], axis=0).astype(BF16)
    for ch in range(tm // CHUNK):
        rows = slice(ch * CHUNK, (ch + 1) * CHUNK)
        res = jnp.dot(w_all, vn[rows], preferred_element_type=F32)
        zc = _head_select(head_of_lane, [res[g * CHUNK:(g + 1) * CHUNK] for g in range(N_HEADS)])
        o_ref[rows, :] = (u[rows] * (zc + bt_ref[...])).astype(BF16)


def _sgu(z, ln_g, ln_b, w_s, b_s):
    m = z.shape[0]
    tm = 512
    bias_tab = jnp.repeat(jnp.transpose(b_s), HEAD_DIM, axis=1)
    return pl.pallas_call(
        functools.partial(_sgu_kernel, tm=tm),
        grid=(m // tm,),
        in_specs=[pl.BlockSpec((tm, GROUP), lambda i: (i, REST_BU)),
                  pl.BlockSpec((tm, GROUP), lambda i: (i, REST_BV)),
                  pl.BlockSpec((1, GROUP), lambda i: (0, 0)),
                  pl.BlockSpec((1, GROUP), lambda i: (0, 0)),
                  pl.BlockSpec((N_HEADS, CHUNK, CHUNK), lambda i: (0, 0, 0)),
                  pl.BlockSpec((CHUNK, GROUP), lambda i: (0, 0))],
        out_specs=pl.BlockSpec((tm, GROUP), lambda i: (i, 0)),
        out_shape=jax.ShapeDtypeStruct((m, GROUP), BF16),
        compiler_params=_cparams(("parallel",)),
        name="sgu",
    )(z, z, ln_g[None, :], ln_b[None, :], w_s, bias_tab)


def _dilated_kernel(*refs):
    q_ref, kc_ref, kp_ref, vc_ref, vp_ref = [refs[2 * t:2 * t + 2] for t in range(5)]
    o_ref = refs[10:12]
    os_ref = [refs[12 + 2 * p:14 + 2 * p] for p in range(3)]
    ls_ref = [refs[18 + 2 * p:20 + 2 * p] for p in range(3)]
    n = pl.program_id(1)
    lane = lax.broadcasted_iota(jnp.int32, (1, GROUP), 1)
    head_of_lane = lane // HEAD_DIM
    i_idx = lax.broadcasted_iota(jnp.int32, (CHUNK, 2 * CHUNK), 0)
    j_idx = lax.broadcasted_iota(jnp.int32, (CHUNK, 2 * CHUNK), 1)
    dist = CHUNK + i_idx - j_idx
    band_bias = jnp.where((dist >= 0) & (dist <= CHUNK), 0.0, NEG_INF)
    prev_half_bias = jnp.where(j_idx >= CHUNK, 0.0, NEG_INF)

    def rows(start, d):
        return pl.ds(start, CHUNK) if d == 1 else pl.ds(start, CHUNK, stride=d)

    def strided(pair, start, d):
        return jnp.concatenate([pair[0][rows(start, d), :], pair[1][rows(start, d), :]], axis=1)

    def store(pair, start, d, val):
        pair[0][rows(start, d), :] = val[:, :128]
        pair[1][rows(start, d), :] = val[:, 128:]

    def scores(d, r, nb, first):
        nblk = SPAN // (CHUNK * d)
        start = nb * (CHUNK * d) + r
        in_prev = (nblk - 1) * (CHUNK * d) + r
        if first is True:
            k_prev, v_prev = strided(kp_ref, in_prev, d), strided(vp_ref, in_prev, d)
        elif first is False:
            k_prev, v_prev = strided(kc_ref, start - CHUNK * d, d), strided(vc_ref, start - CHUNK * d, d)
        else:
            first = nb == 0
            in_span = jnp.maximum(start - CHUNK * d, 0)
            k_prev = jnp.where(first, strided(kp_ref, in_prev, d), strided(kc_ref, in_span, d))
            v_prev = jnp.where(first, strided(vp_ref, in_prev, d), strided(vc_ref, in_span, d))
        qb = strided(q_ref, start, d).astype(BF16)
        kk = jnp.concatenate([k_prev, strided(kc_ref, start, d)], axis=0).astype(BF16)
        vv = jnp.concatenate([v_prev, strided(vc_ref, start, d)], axis=0).astype(BF16)
        q_zero = jnp.zeros_like(qb)
        q_heads = jnp.concatenate([jnp.where(head_of_lane == h, qb, q_zero) for h in range(N_HEADS)], axis=0)
        s_all = lax.dot_general(q_heads, kk, (((1,), (1,)), ((), ())), preferred_element_type=F32)
        ss = [s_all[h * CHUNK:(h + 1) * CHUNK] for h in range(N_HEADS)]
        if first is False:
            bias = band_bias
        elif first is True:
            bias = band_bias + jnp.where(n == 0, prev_half_bias, 0.0)
        else:
            bias = band_bias + jnp.where(first & (n == 0), prev_half_bias, 0.0)
        return ss, vv, bias, start

    def finish(pi, d, ss, vv, bias, start):
        probs, lses = [], []
        for h in range(N_HEADS):
            s = ss[h] + bias
            mx = jnp.max(s, axis=1, keepdims=True)
            p = jnp.exp2(s - mx)
            den = jnp.sum(p, axis=1, keepdims=True)
            probs.append((p / den).astype(BF16))
            lses.append(mx + jnp.log2(den))
        v_zero = jnp.zeros_like(vv)
        v_bd = jnp.concatenate([jnp.where(head_of_lane == h, vv, v_zero) for h in range(N_HEADS)], axis=0)
        ob = jnp.dot(jnp.concatenate(probs, axis=1), v_bd, preferred_element_type=F32)
        lb = _head_select(head_of_lane, lses) + jnp.zeros((CHUNK, GROUP), F32)
        store(os_ref[pi], start, d, ob)
        store(ls_ref[pi], start, d, lb)

    blocks_per_pattern = SPAN // CHUNK
    for pi, d in enumerate(DILATIONS):
        def group(t, carry, pi=pi, d=d):
            trips = blocks_per_pattern // DIL_GROUP
            nblk = blocks_per_pattern // d
            if nblk == 1:
                staged = [scores(d, t + g * trips, 0, True) for g in range(DIL_GROUP)]
            elif nblk == DIL_GROUP:
                staged = [scores(d, t, g, g == 0) for g in range(DIL_GROUP)]
            else:
                assert d == 1
                staged = [scores(d, 0, t + g * trips, None if g == 0 else False) for g in range(DIL_GROUP)]
            for item in staged:
                finish(pi, d, *item)
            return carry

        lax.fori_loop(0, blocks_per_pattern // DIL_GROUP, group, 0)

    for half in range(2):
        l0, l1, l2 = ls_ref[0][half][...], ls_ref[1][half][...], ls_ref[2][half][...]
        mx = jnp.maximum(jnp.maximum(l0, l1), l2)
        w0, w1, w2 = jnp.exp2(l0 - mx), jnp.exp2(l1 - mx), jnp.exp2(l2 - mx)
        tot = w0 + w1 + w2
        o_ref[half][...] = ((w0 / tot) * os_ref[0][half][...] + (w1 / tot) * os_ref[1][half][...]
                            + (w2 / tot) * os_ref[2][half][...]).astype(BF16)


def _dilated(cq, ck, z, batch, seq):
    ns = seq // SPAN
    blk = (SPAN, 128)

    def halves(col, prev):
        def spec(c):
            if prev:
                return pl.BlockSpec(blk, lambda b, n: (b * ns + jnp.maximum(n - 1, 0), c))
            return pl.BlockSpec(blk, lambda b, n: (b * ns + n, c))
        return [spec(2 * col), spec(2 * col + 1)]

    v_col = REST_CV
    return pl.pallas_call(
        _dilated_kernel,
        grid=(batch, ns),
        in_specs=(halves(0, False) + halves(0, False) + halves(0, True)
                  + halves(v_col, False) + halves(v_col, True)),
        out_specs=[pl.BlockSpec(blk, lambda b, n: (b * ns + n, 0))] * 2,
        out_shape=[jax.ShapeDtypeStruct((batch * seq, 128), BF16)] * 2,
        scratch_shapes=[pltpu.VMEM(blk, F32)] * 12,
        compiler_params=_cparams(("parallel", "arbitrary")),
        name="dilated",
    )(cq, cq, ck, ck, ck, ck, z, z, z, z)


def _lane_pick(x, lane, idx):
    return jnp.sum(jnp.where(lane == idx, x, 0.0), axis=1, keepdims=True)


def _mlstm_kernel(*refs, n_seq):
    ins = [refs[5 * p:5 * p + 5] for p in range(n_seq)]
    cw_ref, cb_ref, gb_ref, ng_ref = refs[5 * n_seq:5 * n_seq + 4]
    o_ref = refs[5 * n_seq + 4]
    states = [refs[5 * n_seq + 5 + 4 * p:5 * n_seq + 9 + 4 * p] for p in range(n_seq)]
    c_idx = pl.program_id(0)
    L = CHUNK
    lane = lax.broadcasted_iota(jnp.int32, (1, GROUP), 1)
    head_of_lane = lane // HEAD_DIM
    lane1 = lax.broadcasted_iota(jnp.int32, (1, 128), 1)
    row_head = lax.broadcasted_iota(jnp.int32, (GROUP, 1), 0) // HEAD_DIM
    rows = lax.broadcasted_iota(jnp.int32, (L, 1), 0)
    tril = lax.broadcasted_iota(jnp.int32, (L, L), 1) <= lax.broadcasted_iota(jnp.int32, (L, L), 0)
    tril_ones = jnp.where(tril, 1.0, 0.0)

    @pl.when(c_idx == 0)
    def _():
        for state in states:
            for ref in state:
                ref[...] = jnp.zeros(ref.shape, F32)

    def sequence(p):
        q_ref, k_ref, v_ref, og_ref, gt_ref = ins[p]
        tail_ref, ct_ref, nm_ref, ms_ref = states[p]

        gates = gt_ref[...] + gb_ref[...]
        log_f = jnp.minimum(gates, 0.0) - jnp.log1p(jnp.exp(-jnp.abs(gates)))
        xg = jnp.where(lane1 < N_HEADS, gates, jnp.where(lane1 < 2 * N_HEADS, log_f, 0.0))
        csum = jnp.dot(tril_ones, xg, preferred_element_type=F32,
                       precision=lax.Precision.HIGHEST)

        x = jnp.concatenate([q_ref[...], k_ref[...]], axis=1)
        tail = tail_ref[...]
        pad = jnp.zeros((L - 8, 2 * GROUP), F32)
        y = x * cw_ref[CONV_WIDTH - 1:CONV_WIDTH, :]
        for back in range(1, CONV_WIDTH):
            shifted = jnp.where(rows >= back, pltpu.roll(x, back, 0),
                                jnp.concatenate([pltpu.roll(tail, back, 0), pad], axis=0))
            y = y + shifted * cw_ref[CONV_WIDTH - 1 - back:CONV_WIDTH - back, :]
        tail_ref[...] = x[L - 8:, :]
        qk = jax.nn.silu(y + cb_ref[...])
        mq = qk[:, :GROUP].astype(BF16)
        mk = (qk[:, GROUP:] * (HEAD_DIM ** -0.5)).astype(BF16)
        v = v_ref[...]
        vb = v.astype(BF16)
        yield

        q_zero = jnp.zeros_like(mq)
        q_heads = jnp.concatenate([jnp.where(head_of_lane == h, mq, q_zero) for h in range(N_HEADS)], axis=0)
        qk_all = lax.dot_general(q_heads, mk, (((1,), (1,)), ((), ())), preferred_element_type=F32)
        inter_num = jnp.dot(mq, ct_ref[...].astype(BF16), preferred_element_type=F32)
        nq = jnp.dot(mq, nm_ref[...].astype(BF16), preferred_element_type=F32)
        yield

        xg_t = jnp.transpose(xg)
        csum_t = jnp.transpose(csum)
        m_state = ms_ref[...]
        sws, e_cols, mout_cols, den_cols, w_cols = [], [], [], [], []
        a_sc, e_sc, m_news = [], [], []
        for h in range(N_HEADS):
            b_c = _lane_pick(csum, lane1, N_HEADS + h)
            i_c = _lane_pick(xg, lane1, h)
            b_r = csum_t[N_HEADS + h:N_HEADS + h + 1, :]
            i_r = xg_t[h:h + 1, :]
            b_last = _lane_pick(b_r, lane1, L - 1)
            m_prev = _lane_pick(m_state, lane1, h)
            d_log = jnp.where(tril, b_c - b_r + i_r, NEG_INF)
            g_c = b_last - b_c + i_c
            g_max = jnp.max(g_c, axis=0, keepdims=True)
            w_cols.append(jnp.exp(g_c - g_max))
            inter = b_c + m_prev
            m_out = jnp.maximum(inter, jnp.max(d_log, axis=1, keepdims=True))
            p_mat = jnp.exp(d_log - m_out)
            sw = p_mat * qk_all[h * L:(h + 1) * L]
            e_t = jnp.exp(inter - m_out)
            den_cols.append(jnp.sum(sw, axis=1, keepdims=True) + e_t * _lane_pick(nq, lane1, h))
            sws.append(sw.astype(BF16))
            e_cols.append(e_t)
            mout_cols.append(m_out)
            m_new = jnp.maximum(b_last + m_prev, g_max)
            a_sc.append(jnp.exp(b_last + m_prev - m_new))
            e_sc.append(jnp.exp(g_max - m_new))
            m_news.append(m_new)
        yield

        v_zero = jnp.zeros_like(vb)
        v_bd = jnp.concatenate([jnp.where(head_of_lane == h, vb, v_zero) for h in range(N_HEADS)], axis=0)
        num = (jnp.dot(jnp.concatenate(sws, axis=1), v_bd, preferred_element_type=F32)
               + _head_select(head_of_lane, e_cols) * inter_num)
        wv = (_head_select(head_of_lane, w_cols) * v).astype(BF16)
        ct_loc = lax.dot_general(mk, wv, (((0,), (0,)), ((), ())), preferred_element_type=F32)
        w_mat = jnp.zeros((L, 128), F32)
        for h in range(N_HEADS):
            w_mat = jnp.where(lane1 == h, w_cols[h], w_mat)
        nm_loc = lax.dot_general(mk, w_mat.astype(BF16), (((0,), (0,)), ((), ())), preferred_element_type=F32)
        yield

        den = _head_select(head_of_lane, den_cols)
        hid = num / jnp.maximum(jnp.abs(den), jnp.exp(-_head_select(head_of_lane, mout_cols)))
        mus = [jnp.sum(jnp.where(head_of_lane == h, hid, 0.0), axis=1, keepdims=True) * (1.0 / HEAD_DIM)
               for h in range(N_HEADS)]
        hc = hid - _head_select(head_of_lane, mus)
        hc2 = hc * hc
        var = [jnp.sum(jnp.where(head_of_lane == h, hc2, 0.0), axis=1, keepdims=True) * (1.0 / HEAD_DIM)
               for h in range(N_HEADS)]
        normed = hc * lax.rsqrt(_head_select(head_of_lane, var) + LN_EPS) * ng_ref[...]
        o_ref[p] = (jax.nn.sigmoid(og_ref[...]) * normed).astype(BF16)

        ct_loc = jnp.where(row_head == head_of_lane, ct_loc, 0.0)
        ct_ref[...] = _head_select(row_head, a_sc) * ct_ref[...] + _head_select(row_head, e_sc) * ct_loc
        nm_loc = jnp.where(row_head == lane1, nm_loc, 0.0)
        nm_ref[...] = _head_select(row_head, a_sc) * nm_ref[...] + _head_select(row_head, e_sc) * nm_loc
        m_row = jnp.zeros((1, 128), F32)
        for h in range(N_HEADS):
            m_row = jnp.where(lane1 == h, m_news[h], m_row)
        ms_ref[...] = m_row
        yield

    chains = [sequence(p) for p in range(n_seq)]
    for _ in range(5):
        for chain in chains:
            next(chain)


def _mlstm(z, conv_w, conv_b, gate_b, norm_g, batch, seq):
    nc = seq // CHUNK
    const = lambda c: (0, 0)
    gate_tab = jnp.zeros((1, 128), F32).at[0, :2 * N_HEADS].set(gate_b.reshape(-1))
    seq_specs = []
    for b in range(batch):
        row = lambda col, b=b: (lambda c: (b * nc + c, col))
        seq_specs += [pl.BlockSpec((CHUNK, GROUP), row(REST_DQ)), pl.BlockSpec((CHUNK, GROUP), row(REST_DK)),
                      pl.BlockSpec((CHUNK, GROUP), row(REST_DV)), pl.BlockSpec((CHUNK, GROUP), row(REST_DO)),
                      pl.BlockSpec((CHUNK, 128), row(REST_GATES))]
    state = [pltpu.VMEM((8, 2 * GROUP), F32), pltpu.VMEM((GROUP, GROUP), F32),
             pltpu.VMEM((GROUP, 128), F32), pltpu.VMEM((1, 128), F32)]
    return pl.pallas_call(
        functools.partial(_mlstm_kernel, n_seq=batch),
        grid=(nc,),
        in_specs=seq_specs + [pl.BlockSpec((CONV_WIDTH, 2 * GROUP), const), pl.BlockSpec((1, 2 * GROUP), const),
                              pl.BlockSpec((1, 128), const), pl.BlockSpec((1, GROUP), const)],
        out_specs=pl.BlockSpec((batch, CHUNK, GROUP), lambda c: (0, c, 0)),
        out_shape=jax.ShapeDtypeStruct((batch, seq, GROUP), BF16),
        scratch_shapes=state * batch,
        compiler_params=_cparams(("arbitrary",)),
        name="mlstm",
    )(*([z] * (5 * batch)), conv_w, conv_b[None, :], gate_tab,
      jnp.tile(norm_g, N_HEADS)[None, :]).reshape(batch * seq, GROUP)


def _outproj_kernel(a_ref, b_ref, c_lo_ref, c_hi_ref, d_ref, w_ref, x_ref, g_ref, bb_ref, o_ref):
    mixed = jnp.concatenate([a_ref[...], b_ref[...], c_lo_ref[...], c_hi_ref[...], d_ref[...]], axis=1)
    h = jnp.dot(mixed, w_ref[...], preferred_element_type=F32)
    o_ref[...] = _layer_norm_rows(ALPHA * x_ref[...] + h, g_ref[...], bb_ref[...])


def _outproj(oa, ob, oc, od, w, x2, g, b):
    m = x2.shape[0]
    tm = 512
    part = pl.BlockSpec((tm, GROUP), lambda i: (i, 0))
    half = pl.BlockSpec((tm, 128), lambda i: (i, 0))
    full = pl.BlockSpec((tm, D_MODEL), lambda i: (i, 0))
    vec = pl.BlockSpec((1, D_MODEL), lambda i: (0, 0))
    return pl.pallas_call(
        _outproj_kernel,
        grid=(m // tm,),
        in_specs=[part, part, half, half, part, pl.BlockSpec((D_MODEL, D_MODEL), lambda i: (0, 0)),
                  full, vec, vec],
        out_specs=full,
        out_shape=jax.ShapeDtypeStruct((m, D_MODEL), F32),
        compiler_params=_cparams(("parallel",)),
        name="outproj",
    )(oa, ob, oc[0], oc[1], od, w, x2, g[None, :], b[None, :])


def _ffn_kernel(x_ref, wg_ref, wu_ref, wd_ref, g_ref, b_ref, o_ref):
    x = x_ref[...]
    xb = x.astype(BF16)
    hidden = []
    for c in range(D_FF // FF_CHUNK):
        cols = slice(c * FF_CHUNK, (c + 1) * FF_CHUNK)
        gate = jnp.dot(xb, wg_ref[:, cols], preferred_element_type=F32)
        up = jnp.dot(xb, wu_ref[:, cols], preferred_element_type=F32)
        hidden.append((jax.nn.silu(gate) * up).astype(BF16))
    f = jnp.dot(jnp.concatenate(hidden, axis=1), wd_ref[...], preferred_element_type=F32)
    o_ref[...] = _layer_norm_rows(ALPHA * x + f, g_ref[...], b_ref[...])


def _ffn(x2, wg, wu, wd, g, b):
    m = x2.shape[0]
    tm = 512
    full = pl.BlockSpec((tm, D_MODEL), lambda i: (i, 0))
    vec = pl.BlockSpec((1, D_MODEL), lambda i: (0, 0))
    return pl.pallas_call(
        _ffn_kernel,
        grid=(m // tm,),
        in_specs=[full, pl.BlockSpec((D_MODEL, D_FF), lambda i: (0, 0)),
                  pl.BlockSpec((D_MODEL, D_FF), lambda i: (0, 0)),
                  pl.BlockSpec((D_FF, D_MODEL), lambda i: (0, 0)), vec, vec],
        out_specs=full,
        out_shape=jax.ShapeDtypeStruct((m, D_MODEL), F32),
        compiler_params=_cparams(("parallel",)),
        name="ffn",
    )(x2, wg, wu, wd, g[None, :], b[None, :])


def kernel(x, w_in, diff_lambda, diff_subln_g, sgu_ln_g, sgu_ln_b, sgu_w, sgu_b, mlstm_conv_w, mlstm_conv_b,
           mlstm_gate_b, mlstm_norm_g, w_out, ln1_g, ln1_b, w_gate, w_up, w_down, ln2_g, ln2_b):
    batch, seq, _ = x.shape
    assert seq % SPAN == 0
    x2 = x.reshape(batch * seq, D_MODEL)
    tab_a = _rope_table(seq, DIFF_QK, 128 // DIFF_QK)
    tab_c = _rope_table(seq, HEAD_DIM, 128 // HEAD_DIM)
    w_in_p = jnp.pad(w_in, ((0, 0), (0, 0), (0, IN_PAD - IN_WIDTH))).astype(BF16)
    w_out_b = w_out.astype(BF16)
    w_gate_b, w_up_b, w_down_b = w_gate.astype(BF16), w_up.astype(BF16), w_down.astype(BF16)
    for l in range(DEPTH):
        lam_init = 0.8 - 0.6 * math.exp(-0.3 * l)
        z, aq, ak, av, cq, ck = _inproj(x2, w_in_p[l], tab_a, tab_c, seq)
        out_a = _diff_attn(aq, ak, av, diff_lambda[l], diff_subln_g[l], lam_init, batch, seq)
        out_b = _sgu(z, sgu_ln_g[l], sgu_ln_b[l], sgu_w[l], sgu_b[l])
        out_c = _dilated(cq, ck, z, batch, seq)
        out_d = _mlstm(z, mlstm_conv_w[l], mlstm_conv_b[l], mlstm_gate_b[l], mlstm_norm_g[l], batch, seq)
        x2 = _outproj(out_a, out_b, out_c, out_d, w_out_b[l], x2, ln1_g[l], ln1_b[l])
        x2 = _ffn(x2, w_gate_b[l], w_up_b[l], w_down_b[l], ln2_g[l], ln2_b[l])
    return x2.reshape(batch, seq, D_MODEL)
```

```python
import functools
import math

import jax
import jax.numpy as jnp
from jax import lax
from jax.experimental import pallas as pl
from jax.experimental.pallas import tpu as pltpu

F32 = jnp.float32
BF16 = jnp.bfloat16

D_MODEL = 1024
DEPTH = 4
N_HEADS = 4
HEAD_DIM = 64
GROUP = N_HEADS * HEAD_DIM
DIFF_QK = HEAD_DIM // 2
CHUNK = 128
DILATIONS = (1, 4, 16)
SPAN = CHUNK * DILATIONS[-1]
DIL_GROUP = 4
CONV_WIDTH = 4
ROPE_THETA = 500000.0
D_FF = 2816
FF_CHUNK = 256
IN_WIDTH = 3080
IN_PAD = 3200
COL_AQ, COL_AK, COL_AV, COL_CQ, COL_CK = 0, 256, 512, 1280, 1536
REST_CHUNKS = ((768, 0, 512), (1792, 512, 512), (2304, 1024, 512), (2816, 1536, 384))
REST_WIDTH = 1920
REST_BU, REST_BV, REST_CV, REST_DQ, REST_DK, REST_DV, REST_DO = range(7)
REST_GATES = 1792 // 128
ATT_TILE = 256
ATT_TQ = 2 * ATT_TILE
ACC_ROWS = HEAD_DIM + 16
LOG2E = math.log2(math.e)
ALPHA = (2 * DEPTH) ** 0.25
LN_EPS = 1e-5
NEG_INF = float("-inf")
VMEM_LIMIT = 56 * 1024 * 1024


def _cparams(sem):
    return pltpu.CompilerParams(dimension_semantics=sem, vmem_limit_bytes=VMEM_LIMIT)


def _head_select(head_of_lane, parts):
    out = parts[-1]
    for h in range(len(parts) - 2, -1, -1):
        out = jnp.where(head_of_lane == h, parts[h], out)
    return out


def _layer_norm_rows(y, g, b):
    mu = jnp.mean(y, axis=-1, keepdims=True)
    yc = y - mu
    var = jnp.mean(yc * yc, axis=-1, keepdims=True)
    return yc * lax.rsqrt(var + LN_EPS) * g + b


def _rope(x, c, s_lo, s_hi, half):
    width = x.shape[1]
    return x * c + pltpu.roll(x, width - half, 1) * s_lo + pltpu.roll(x, half, 1) * s_hi


def _inproj_kernel(x_ref, w_ref, ta_ref, tc_ref, rest_ref, oaq_ref, oak_ref, oav_ref, ocq_ref, ock_ref):
    xb = x_ref[...].astype(BF16)

    def project(col0, width=GROUP):
        return jnp.dot(xb, w_ref[:, col0:col0 + width], preferred_element_type=F32)

    ta_c, ta_lo, ta_hi = [jnp.concatenate([ta_ref[t]] * (GROUP // 128), axis=1) for t in range(3)]
    tc_c, tc_lo, tc_hi = [jnp.concatenate([tc_ref[t]] * (GROUP // 128), axis=1) for t in range(3)]
    half_a = DIFF_QK // 8
    half_c = HEAD_DIM // 8
    q = _rope(project(COL_AQ), ta_c, ta_lo, ta_hi, half_a) * (DIFF_QK ** -0.5 * LOG2E)
    oak_ref[...] = _rope(project(COL_AK), ta_c, ta_lo, ta_hi, half_a).astype(BF16)
    v = project(COL_AV)
    for t in range(q.shape[0] // ATT_TILE):
        rows = slice(t * ATT_TILE, (t + 1) * ATT_TILE)
        oaq_ref[t] = jnp.transpose(q[rows])
        oav_ref[t] = jnp.transpose(v[rows]).astype(BF16)
    ocq_ref[...] = _rope(project(COL_CQ), tc_c, tc_lo, tc_hi, half_c) * (HEAD_DIM ** -0.5 * LOG2E)
    ock_ref[...] = _rope(project(COL_CK), tc_c, tc_lo, tc_hi, half_c)
    for src, dst, width in REST_CHUNKS:
        rest_ref[:, dst:dst + width] = project(src, width)


def _inproj(x2, w, tab_a, tab_c, seq, layer):
    m = x2.shape[0]
    tm = 512
    nt = seq // tm
    tspec = pl.BlockSpec((3, tm, 128), lambda i: (0, i % nt, 0))
    ospec = pl.BlockSpec((tm, GROUP), lambda i: (i, 0))
    tspec_out = pl.BlockSpec((tm // ATT_TILE, GROUP, ATT_TILE), lambda i: (i, 0, 0))
    transposed = lambda dt: jax.ShapeDtypeStruct((m // ATT_TILE, GROUP, ATT_TILE), dt)
    return pl.pallas_call(
        _inproj_kernel,
        grid=(m // tm,),
        in_specs=[pl.BlockSpec((tm, D_MODEL), lambda i: (i, 0)),
                  _layer_weight(layer, (D_MODEL, IN_PAD)), tspec, tspec],
        out_specs=[pl.BlockSpec((tm, REST_WIDTH), lambda i: (i, 0)), tspec_out, ospec, tspec_out, ospec, ospec],
        out_shape=[jax.ShapeDtypeStruct((m, REST_WIDTH), F32), transposed(F32),
                   jax.ShapeDtypeStruct((m, GROUP), BF16), transposed(BF16),
                   jax.ShapeDtypeStruct((m, GROUP), F32), jax.ShapeDtypeStruct((m, GROUP), F32)],
        compiler_params=_cparams(("parallel",)),
        name="inproj",
    )(x2, w, tab_a, tab_c)


def _rope_table(seq, group_dim, n_groups):
    rot = group_dim // 4
    half = rot // 2
    pos = jnp.arange(seq, dtype=F32)
    inv = ROPE_THETA ** (-jnp.arange(0, rot, 2, dtype=F32) / rot)
    ang = pos[:, None] * inv[None, :]
    cos, sin = jnp.cos(ang), jnp.sin(ang)
    ones = jnp.ones((seq, group_dim - rot), F32)
    zeros_half = jnp.zeros((seq, half), F32)
    zeros_rest = jnp.zeros((seq, group_dim - rot), F32)
    c = jnp.concatenate([cos, cos, ones], axis=1)
    lo = jnp.concatenate([-sin, zeros_half, zeros_rest], axis=1)
    hi = jnp.concatenate([zeros_half, sin, zeros_rest], axis=1)
    return jnp.stack([jnp.tile(t, (1, n_groups)) for t in (c, lo, hi)], axis=0)


def _diff_attn_kernel(lamv_ref, g_ref, qt_ref, k_ref, vt_ref, o_ref, qm_ref, m_ref, acc_ref,
                      sa_ref, sb_ref, xa_ref, xb_ref, *, lam_init):
    tq, tk = ATT_TQ, ATT_TILE
    i = pl.program_id(1)
    map_of_row = lax.broadcasted_iota(jnp.int32, (GROUP, 1), 0) // DIFF_QK
    qt = jnp.concatenate([qt_ref[c] for c in range(tq // ATT_TILE)], axis=1)
    for hm in range(2 * N_HEADS):
        qm_ref[hm] = jnp.where(map_of_row == hm, qt, 0.0).astype(BF16)
    m_ref[...] = jnp.full(m_ref.shape, NEG_INF, F32)
    acc_ref[...] = jnp.zeros(acc_ref.shape, F32)
    ones = jnp.ones((ACC_ROWS - HEAD_DIM, tk), BF16)

    def key_tile(j):
        return k_ref[pl.ds(pl.multiple_of(j * tk, tk), tk), :]

    def produce(dst, j, hm):
        scores = jnp.dot(key_tile(j), qm_ref[hm], preferred_element_type=F32)
        dst[0][hm] = scores
        dst[1][hm:hm + 1, :] = jnp.max(scores, axis=0, keepdims=True)

    def consume(src, j, hm, masked):
        s = src[0][hm]
        if masked:
            key_pos = j * tk + lax.broadcasted_iota(jnp.int32, (tk, tq), 0)
            q_pos = i * tq + lax.broadcasted_iota(jnp.int32, (tk, tq), 1)
            s = jnp.where(key_pos <= q_pos, s, NEG_INF)
            m_tile = jnp.max(s, axis=0, keepdims=True)
        else:
            m_tile = src[1][hm:hm + 1, :]
        m_prev = m_ref[hm:hm + 1, :]
        m_new = jnp.maximum(m_prev, m_tile)
        alpha = jnp.exp2(m_prev - m_new)
        p = jnp.exp2(s - m_new).astype(BF16)
        h = hm // 2
        v_ext = jnp.concatenate([vt_ref[j, h * HEAD_DIM:(h + 1) * HEAD_DIM, :], ones], axis=0)
        acc_ref[hm] = alpha * acc_ref[hm] + jnp.dot(v_ext, p, preferred_element_type=F32)
        m_ref[hm:hm + 1, :] = m_new

    def stage(src, j, dst, masked):
        for hm in range(2 * N_HEADS):
            if dst is not None:
                produce(dst, j + 1, hm)
            consume(src, j, hm, masked)

    buf_a, buf_b = (sa_ref, xa_ref), (sb_ref, xb_ref)
    for hm in range(2 * N_HEADS):
        produce(buf_a, 0, hm)

    n_plain = i

    def pair(first):
        stage(buf_a, first, buf_b, False)
        stage(buf_b, first + 1, buf_a, False)

    def two_pairs(u, carry):
        pair(4 * u)
        pair(4 * u + 2)
        return carry

    def one_pair(u, carry):
        pair(2 * u)
        return carry

    lax.fori_loop(0, n_plain // 2, two_pairs, 0)
    lax.fori_loop(2 * (n_plain // 2), n_plain, one_pair, 0)
    last = 2 * n_plain
    stage(buf_a, last, buf_b, True)
    stage(buf_b, last + 1, None, True)

    lv = lamv_ref[...]
    lam = (jnp.exp(jnp.sum(lv[0:1] * lv[1:2], axis=1, keepdims=True))
           - jnp.exp(jnp.sum(lv[2:3] * lv[3:4], axis=1, keepdims=True)) + lam_init)
    outs = []
    for h in range(N_HEADS):
        a0, a1 = acc_ref[2 * h], acc_ref[2 * h + 1]
        a = a0[:HEAD_DIM] / a0[HEAD_DIM:HEAD_DIM + 1] - lam * (a1[:HEAD_DIM] / a1[HEAD_DIM:HEAD_DIM + 1])
        ms = jnp.mean(a * a, axis=0, keepdims=True)
        outs.append(a * lax.rsqrt(ms + LN_EPS))
    o_ref[...] = (jnp.transpose(jnp.concatenate(outs, axis=0)) * g_ref[...] * (1.0 - lam_init)).astype(BF16)


def _diff_attn(aq_t, ak, av_t, lam_vecs, subln_g, lam_init, batch, seq):
    tq, tk = ATT_TQ, ATT_TILE
    nq = seq // tq
    maps = 2 * N_HEADS
    return pl.pallas_call(
        functools.partial(_diff_attn_kernel, lam_init=lam_init),
        grid=(batch, nq),
        in_specs=[pl.BlockSpec((4, DIFF_QK), lambda b, i: (0, 0)),
                  pl.BlockSpec((1, GROUP), lambda b, i: (0, 0)),
                  pl.BlockSpec((tq // tk, GROUP, tk), lambda b, i: (b * nq + i, 0, 0)),
                  pl.BlockSpec((seq, GROUP), lambda b, i: (b, 0)),
                  pl.BlockSpec((seq // tk, GROUP, tk), lambda b, i: (b, 0, 0))],
        out_specs=pl.BlockSpec((tq, GROUP), lambda b, i: (b * nq + i, 0)),
        out_shape=jax.ShapeDtypeStruct((batch * seq, GROUP), BF16),
        scratch_shapes=[pltpu.VMEM((maps, GROUP, tq), BF16),
                        pltpu.VMEM((maps, tq), F32),
                        pltpu.VMEM((maps, ACC_ROWS, tq), F32),
                        pltpu.VMEM((maps, tk, tq), F32), pltpu.VMEM((maps, tk, tq), F32),
                        pltpu.VMEM((maps, tq), F32), pltpu.VMEM((maps, tq), F32)],
        compiler_params=_cparams(("parallel", "arbitrary")),
        name="diff_attn",
    )(lam_vecs, jnp.tile(subln_g, N_HEADS)[None, :], aq_t, ak, av_t)


def _sgu_kernel(u_ref, v_ref, g_ref, b_ref, w_ref, bt_ref, o_ref, *, tm):
    lane = lax.broadcasted_iota(jnp.int32, (1, GROUP), 1)
    head_of_lane = lane // HEAD_DIM
    u = jax.nn.gelu(u_ref[...])
    vn = _layer_norm_rows(jax.nn.gelu(v_ref[...]), g_ref[...], b_ref[...]).astype(BF16)
    r = lax.broadcasted_iota(jnp.int32, (CHUNK, CHUNK), 0)
    c = lax.broadcasted_iota(jnp.int32, (CHUNK, CHUNK), 1)
    w_all = jnp.concatenate([jnp.where(c <= r, w_ref[g], 0.0) for g in range(N_HEADS)], axis=0).astype(BF16)
    for ch in range(tm // CHUNK):
        rows = slice(ch * CHUNK, (ch + 1) * CHUNK)
        res = jnp.dot(w_all, vn[rows], preferred_element_type=F32)
        zc = _head_select(head_of_lane, [res[g * CHUNK:(g + 1) * CHUNK] for g in range(N_HEADS)])
        o_ref[rows, :] = (u[rows] * (zc + bt_ref[...])).astype(BF16)


def _sgu(z, ln_g, ln_b, w_s, b_s):
    m = z.shape[0]
    tm = 512
    bias_tab = jnp.repeat(jnp.transpose(b_s), HEAD_DIM, axis=1)
    return pl.pallas_call(
        functools.partial(_sgu_kernel, tm=tm),
        grid=(m // tm,),
        in_specs=[pl.BlockSpec((tm, GROUP), lambda i: (i, REST_BU)),
                  pl.BlockSpec((tm, GROUP), lambda i: (i, REST_BV)),
                  pl.BlockSpec((1, GROUP), lambda i: (0, 0)),
                  pl.BlockSpec((1, GROUP), lambda i: (0, 0)),
                  pl.BlockSpec((N_HEADS, CHUNK, CHUNK), lambda i: (0, 0, 0)),
                  pl.BlockSpec((CHUNK, GROUP), lambda i: (0, 0))],
        out_specs=pl.BlockSpec((tm, GROUP), lambda i: (i, 0)),
        out_shape=jax.ShapeDtypeStruct((m, GROUP), BF16),
        compiler_params=_cparams(("parallel",)),
        name="sgu",
    )(z, z, ln_g[None, :], ln_b[None, :], w_s, bias_tab)


def _dilated_kernel(*refs):
    q_ref, kc_ref, kp_ref, vc_ref, vp_ref = [refs[2 * t:2 * t + 2] for t in range(5)]
    o_ref = refs[10:12]
    os_ref = [refs[12 + 2 * p:14 + 2 * p] for p in range(3)]
    ls_ref = [refs[18 + 2 * p:20 + 2 * p] for p in range(3)]
    n = pl.program_id(1)
    lane = lax.broadcasted_iota(jnp.int32, (1, GROUP), 1)
    head_of_lane = lane // HEAD_DIM
    i_idx = lax.broadcasted_iota(jnp.int32, (CHUNK, 2 * CHUNK), 0)
    j_idx = lax.broadcasted_iota(jnp.int32, (CHUNK, 2 * CHUNK), 1)
    dist = CHUNK + i_idx - j_idx
    band_bias = jnp.where((dist >= 0) & (dist <= CHUNK), 0.0, NEG_INF)
    prev_half_bias = jnp.where(j_idx >= CHUNK, 0.0, NEG_INF)

    def rows(start, d):
        return pl.ds(start, CHUNK) if d == 1 else pl.ds(start, CHUNK, stride=d)

    def strided(pair, start, d):
        return jnp.concatenate([pair[0][rows(start, d), :], pair[1][rows(start, d), :]], axis=1)

    def store(pair, start, d, val):
        pair[0][rows(start, d), :] = val[:, :128]
        pair[1][rows(start, d), :] = val[:, 128:]

    def scores(d, r, nb, first):
        nblk = SPAN // (CHUNK * d)
        start = nb * (CHUNK * d) + r
        in_prev = (nblk - 1) * (CHUNK * d) + r
        if first is True:
            k_prev, v_prev = strided(kp_ref, in_prev, d), strided(vp_ref, in_prev, d)
        elif first is False:
            k_prev, v_prev = strided(kc_ref, start - CHUNK * d, d), strided(vc_ref, start - CHUNK * d, d)
        else:
            first = nb == 0
            in_span = jnp.maximum(start - CHUNK * d, 0)
            k_prev = jnp.where(first, strided(kp_ref, in_prev, d), strided(kc_ref, in_span, d))
            v_prev = jnp.where(first, strided(vp_ref, in_prev, d), strided(vc_ref, in_span, d))
        qb = strided(q_ref, start, d).astype(BF16)
        kk = jnp.concatenate([k_prev, strided(kc_ref, start, d)], axis=0).astype(BF16)
        vv = jnp.concatenate([v_prev, strided(vc_ref, start, d)], axis=0).astype(BF16)
        q_zero = jnp.zeros_like(qb)
        q_heads = jnp.concatenate([jnp.where(head_of_lane == h, qb, q_zero) for h in range(N_HEADS)], axis=0)
        s_all = lax.dot_general(q_heads, kk, (((1,), (1,)), ((), ())), preferred_element_type=F32)
        ss = [s_all[h * CHUNK:(h + 1) * CHUNK] for h in range(N_HEADS)]
        if first is False:
            bias = band_bias
        elif first is True:
            bias = band_bias + jnp.where(n == 0, prev_half_bias, 0.0)
        else:
            bias = band_bias + jnp.where(first & (n == 0), prev_half_bias, 0.0)
        return ss, vv, bias, start

    def finish(pi, d, ss, vv, bias, start):
        probs, lses = [], []
        for h in range(N_HEADS):
            s = ss[h] + bias
            mx = jnp.max(s, axis=1, keepdims=True)
            p = jnp.exp2(s - mx)
            den = jnp.sum(p, axis=1, keepdims=True)
            probs.append((p / den).astype(BF16))
            lses.append(mx + jnp.log2(den))
        v_zero = jnp.zeros_like(vv)
        v_bd = jnp.concatenate([jnp.where(head_of_lane == h, vv, v_zero) for h in range(N_HEADS)], axis=0)
        ob = jnp.dot(jnp.concatenate(probs, axis=1), v_bd, preferred_element_type=F32)
        lb = _head_select(head_of_lane, lses) + jnp.zeros((CHUNK, GROUP), F32)
        store(os_ref[pi], start, d, ob)
        store(ls_ref[pi], start, d, lb)

    blocks_per_pattern = SPAN // CHUNK
    for pi, d in enumerate(DILATIONS):
        def group(t, carry, pi=pi, d=d):
            trips = blocks_per_pattern // DIL_GROUP
            nblk = blocks_per_pattern // d
            if nblk == 1:
                staged = [scores(d, t + g * trips, 0, True) for g in range(DIL_GROUP)]
            elif nblk == DIL_GROUP:
                staged = [scores(d, t, g, g == 0) for g in range(DIL_GROUP)]
            else:
                assert d == 1
                staged = [scores(d, 0, t + g * trips, None if g == 0 else False) for g in range(DIL_GROUP)]
            for item in staged:
                finish(pi, d, *item)
            return carry

        lax.fori_loop(0, blocks_per_pattern // DIL_GROUP, group, 0)

    for half in range(2):
        l0, l1, l2 = ls_ref[0][half][...], ls_ref[1][half][...], ls_ref[2][half][...]
        mx = jnp.maximum(jnp.maximum(l0, l1), l2)
        w0, w1, w2 = jnp.exp2(l0 - mx), jnp.exp2(l1 - mx), jnp.exp2(l2 - mx)
        tot = w0 + w1 + w2
        o_ref[half][...] = ((w0 / tot) * os_ref[0][half][...] + (w1 / tot) * os_ref[1][half][...]
                            + (w2 / tot) * os_ref[2][half][...]).astype(BF16)


def _dilated(cq, ck, z, batch, seq):
    ns = seq // SPAN
    blk = (SPAN, 128)

    def halves(col, prev):
        def spec(c):
            if prev:
                return pl.BlockSpec(blk, lambda b, n: (b * ns + jnp.maximum(n - 1, 0), c))
            return pl.BlockSpec(blk, lambda b, n: (b * ns + n, c))
        return [spec(2 * col), spec(2 * col + 1)]

    v_col = REST_CV
    return pl.pallas_call(
        _dilated_kernel,
        grid=(batch, ns),
        in_specs=(halves(0, False) + halves(0, False) + halves(0, True)
                  + halves(v_col, False) + halves(v_col, True)),
        out_specs=[pl.BlockSpec(blk, lambda b, n: (b * ns + n, 0))] * 2,
        out_shape=[jax.ShapeDtypeStruct((batch * seq, 128), BF16)] * 2,
        scratch_shapes=[pltpu.VMEM(blk, F32)] * 12,
        compiler_params=_cparams(("parallel", "arbitrary")),
        name="dilated",
    )(cq, cq, ck, ck, ck, ck, z, z, z, z)


def _lane_pick(x, lane, idx):
    return jnp.sum(jnp.where(lane == idx, x, 0.0), axis=1, keepdims=True)


def _mlstm_kernel(*refs, n_seq):
    ins = [refs[5 * p:5 * p + 5] for p in range(n_seq)]
    cw_ref, cb_ref, gb_ref, ng_ref = refs[5 * n_seq:5 * n_seq + 4]
    o_ref = refs[5 * n_seq + 4]
    states = [refs[5 * n_seq + 5 + 4 * p:5 * n_seq + 9 + 4 * p] for p in range(n_seq)]
    c_idx = pl.program_id(0)
    L = CHUNK
    lane = lax.broadcasted_iota(jnp.int32, (1, GROUP), 1)
    head_of_lane = lane // HEAD_DIM
    lane1 = lax.broadcasted_iota(jnp.int32, (1, 128), 1)
    row_head = lax.broadcasted_iota(jnp.int32, (GROUP, 1), 0) // HEAD_DIM
    rows = lax.broadcasted_iota(jnp.int32, (L, 1), 0)
    tril = lax.broadcasted_iota(jnp.int32, (L, L), 1) <= lax.broadcasted_iota(jnp.int32, (L, L), 0)
    tril_ones = jnp.where(tril, 1.0, 0.0)

    @pl.when(c_idx == 0)
    def _():
        for state in states:
            for ref in state:
                ref[...] = jnp.zeros(ref.shape, F32)

    def sequence(p):
        q_ref, k_ref, v_ref, og_ref, gt_ref = ins[p]
        tail_ref, ct_ref, nm_ref, ms_ref = states[p]

        gates = gt_ref[...] + gb_ref[...]
        log_f = jnp.minimum(gates, 0.0) - jnp.log1p(jnp.exp(-jnp.abs(gates)))
        xg = jnp.where(lane1 < N_HEADS, gates, jnp.where(lane1 < 2 * N_HEADS, log_f, 0.0))
        csum = jnp.dot(tril_ones, xg, preferred_element_type=F32,
                       precision=lax.Precision.HIGHEST)

        x = jnp.concatenate([q_ref[...], k_ref[...]], axis=1)
        tail = tail_ref[...]
        pad = jnp.zeros((L - 8, 2 * GROUP), F32)
        y = x * cw_ref[CONV_WIDTH - 1:CONV_WIDTH, :]
        for back in range(1, CONV_WIDTH):
            shifted = jnp.where(rows >= back, pltpu.roll(x, back, 0),
                                jnp.concatenate([pltpu.roll(tail, back, 0), pad], axis=0))
            y = y + shifted * cw_ref[CONV_WIDTH - 1 - back:CONV_WIDTH - back, :]
        tail_ref[...] = x[L - 8:, :]
        qk = jax.nn.silu(y + cb_ref[...])
        mq = qk[:, :GROUP].astype(BF16)
        mk = (qk[:, GROUP:] * (HEAD_DIM ** -0.5)).astype(BF16)
        v = v_ref[...]
        vb = v.astype(BF16)
        yield

        q_zero = jnp.zeros_like(mq)
        q_heads = jnp.concatenate([jnp.where(head_of_lane == h, mq, q_zero) for h in range(N_HEADS)], axis=0)
        qk_all = lax.dot_general(q_heads, mk, (((1,), (1,)), ((), ())), preferred_element_type=F32)
        inter_num = jnp.dot(mq, ct_ref[...].astype(BF16), preferred_element_type=F32)
        nq = jnp.dot(mq, nm_ref[...].astype(BF16), preferred_element_type=F32)
        yield

        xg_t = jnp.transpose(xg)
        csum_t = jnp.transpose(csum)
        m_state = ms_ref[...]
        sws, e_cols, mout_cols, den_cols, w_cols = [], [], [], [], []
        a_sc, e_sc, m_news = [], [], []
        for h in range(N_HEADS):
            b_c = _lane_pick(csum, lane1, N_HEADS + h)
            i_c = _lane_pick(xg, lane1, h)
            b_r = csum_t[N_HEADS + h:N_HEADS + h + 1, :]
            i_r = xg_t[h:h + 1, :]
            b_last = _lane_pick(b_r, lane1, L - 1)
            m_prev = _lane_pick(m_state, lane1, h)
            d_log = jnp.where(tril, b_c - b_r + i_r, NEG_INF)
            g_c = b_last - b_c + i_c
            g_max = jnp.max(g_c, axis=0, keepdims=True)
            w_cols.append(jnp.exp(g_c - g_max))
            inter = b_c + m_prev
            m_out = jnp.maximum(inter, jnp.max(d_log, axis=1, keepdims=True))
            p_mat = jnp.exp(d_log - m_out)
            sw = p_mat * qk_all[h * L:(h + 1) * L]
            e_t = jnp.exp(inter - m_out)
            den_cols.append(jnp.sum(sw, axis=1, keepdims=True) + e_t * _lane_pick(nq, lane1, h))
            sws.append(sw.astype(BF16))
            e_cols.append(e_t)
            mout_cols.append(m_out)
            m_new = jnp.maximum(b_last + m_prev, g_max)
            a_sc.append(jnp.exp(b_last + m_prev - m_new))
            e_sc.append(jnp.exp(g_max - m_new))
            m_news.append(m_new)
        yield

        v_zero = jnp.zeros_like(vb)
        v_bd = jnp.concatenate([jnp.where(head_of_lane == h, vb, v_zero) for h in range(N_HEADS)], axis=0)
        num = (jnp.dot(jnp.concatenate(sws, axis=1), v_bd, preferred_element_type=F32)
               + _head_select(head_of_lane, e_cols) * inter_num)
        wv = (_head_select(head_of_lane, w_cols) * v).astype(BF16)
        ct_loc = lax.dot_general(mk, wv, (((0,), (0,)), ((), ())), preferred_element_type=F32)
        w_mat = jnp.zeros((L, 128), F32)
        for h in range(N_HEADS):
            w_mat = jnp.where(lane1 == h, w_cols[h], w_mat)
        nm_loc = lax.dot_general(mk, w_mat.astype(BF16), (((0,), (0,)), ((), ())), preferred_element_type=F32)
        yield

        den = _head_select(head_of_lane, den_cols)
        hid = num / jnp.maximum(jnp.abs(den), jnp.exp(-_head_select(head_of_lane, mout_cols)))
        mus = [jnp.sum(jnp.where(head_of_lane == h, hid, 0.0), axis=1, keepdims=True) * (1.0 / HEAD_DIM)
               for h in range(N_HEADS)]
        hc = hid - _head_select(head_of_lane, mus)
        hc2 = hc * hc
        var = [jnp.sum(jnp.where(head_of_lane == h, hc2, 0.0), axis=1, keepdims=True) * (1.0 / HEAD_DIM)
               for h in range(N_HEADS)]
        normed = hc * lax.rsqrt(_head_select(head_of_lane, var) + LN_EPS) * ng_ref[...]
        o_ref[p] = (jax.nn.sigmoid(og_ref[...]) * normed).astype(BF16)

        ct_loc = jnp.where(row_head == head_of_lane, ct_loc, 0.0)
        ct_ref[...] = _head_select(row_head, a_sc) * ct_ref[...] + _head_select(row_head, e_sc) * ct_loc
        nm_loc = jnp.where(row_head == lane1, nm_loc, 0.0)
        nm_ref[...] = _head_select(row_head, a_sc) * nm_ref[...] + _head_select(row_head, e_sc) * nm_loc
        m_row = jnp.zeros((1, 128), F32)
        for h in range(N_HEADS):
            m_row = jnp.where(lane1 == h, m_news[h], m_row)
        ms_ref[...] = m_row
        yield

    chains = [sequence(p) for p in range(n_seq)]
    for _ in range(5):
        for chain in chains:
            next(chain)


def _mlstm(z, conv_w, conv_b, gate_b, norm_g, batch, seq):
    nc = seq // CHUNK
    const = lambda c: (0, 0)
    gate_tab = jnp.zeros((1, 128), F32).at[0, :2 * N_HEADS].set(gate_b.reshape(-1))
    seq_specs = []
    for b in range(batch):
        row = lambda col, b=b: (lambda c: (b * nc + c, col))
        seq_specs += [pl.BlockSpec((CHUNK, GROUP), row(REST_DQ)), pl.BlockSpec((CHUNK, GROUP), row(REST_DK)),
                      pl.BlockSpec((CHUNK, GROUP), row(REST_DV)), pl.BlockSpec((CHUNK, GROUP), row(REST_DO)),
                      pl.BlockSpec((CHUNK, 128), row(REST_GATES))]
    state = [pltpu.VMEM((8, 2 * GROUP), F32), pltpu.VMEM((GROUP, GROUP), F32),
             pltpu.VMEM((GROUP, 128), F32), pltpu.VMEM((1, 128), F32)]
    return pl.pallas_call(
        functools.partial(_mlstm_kernel, n_seq=batch),
        grid=(nc,),
        in_specs=seq_specs + [pl.BlockSpec((CONV_WIDTH, 2 * GROUP), const), pl.BlockSpec((1, 2 * GROUP), const),
                              pl.BlockSpec((1, 128), const), pl.BlockSpec((1, GROUP), const)],
        out_specs=pl.BlockSpec((batch, CHUNK, GROUP), lambda c: (0, c, 0)),
        out_shape=jax.ShapeDtypeStruct((batch, seq, GROUP), BF16),
        scratch_shapes=state * batch,
        compiler_params=_cparams(("arbitrary",)),
        name="mlstm",
    )(*([z] * (5 * batch)), conv_w, conv_b[None, :], gate_tab,
      jnp.tile(norm_g, N_HEADS)[None, :]).reshape(batch * seq, GROUP)


def _mix_ffn_kernel(a_ref, b_ref, c_lo_ref, c_hi_ref, d_ref, wo_ref, x_ref, g1_ref, b1_ref,
                    wg_ref, wu_ref, wd_ref, g2_ref, b2_ref, o_ref):
    mixed = jnp.concatenate([a_ref[...], b_ref[...], c_lo_ref[...], c_hi_ref[...], d_ref[...]], axis=1)
    h = jnp.dot(mixed, wo_ref[...], preferred_element_type=F32)
    x1 = _layer_norm_rows(ALPHA * x_ref[...] + h, g1_ref[...], b1_ref[...])
    xb = x1.astype(BF16)
    hidden = []
    for c in range(D_FF // FF_CHUNK):
        cols = slice(c * FF_CHUNK, (c + 1) * FF_CHUNK)
        gate = jnp.dot(xb, wg_ref[:, cols], preferred_element_type=F32)
        up = jnp.dot(xb, wu_ref[:, cols], preferred_element_type=F32)
        hidden.append((jax.nn.silu(gate) * up).astype(BF16))
    f = jnp.dot(jnp.concatenate(hidden, axis=1), wd_ref[...], preferred_element_type=F32)
    o_ref[...] = _layer_norm_rows(ALPHA * x1 + f, g2_ref[...], b2_ref[...])


def _layer_weight(layer, shape):
    return pl.BlockSpec((None,) + shape, lambda i: (layer, 0, 0), pipeline_mode=pl.Buffered(1))


def _mix_ffn(oa, ob, oc, od, wo, x2, g1, b1, wg, wu, wd, g2, b2, layer):
    m = x2.shape[0]
    tm = 512
    part = pl.BlockSpec((tm, GROUP), lambda i: (i, 0))
    half = pl.BlockSpec((tm, 128), lambda i: (i, 0))
    full = pl.BlockSpec((tm, D_MODEL), lambda i: (i, 0))
    vec = pl.BlockSpec((1, D_MODEL), lambda i: (0, 0))
    whole = functools.partial(_layer_weight, layer)
    return pl.pallas_call(
        _mix_ffn_kernel,
        grid=(m // tm,),
        in_specs=[part, part, half, half, part, whole((D_MODEL, D_MODEL)), full, vec, vec,
                  whole((D_MODEL, D_FF)), whole((D_MODEL, D_FF)), whole((D_FF, D_MODEL)), vec, vec],
        out_specs=full,
        out_shape=jax.ShapeDtypeStruct((m, D_MODEL), F32),
        compiler_params=_cparams(("parallel",)),
        name="mix_ffn",
    )(oa, ob, oc[0], oc[1], od, wo, x2, g1[None, :], b1[None, :], wg, wu, wd, g2[None, :], b2[None, :])


def kernel(x, w_in, diff_lambda, diff_subln_g, sgu_ln_g, sgu_ln_b, sgu_w, sgu_b, mlstm_conv_w, mlstm_conv_b,
           mlstm_gate_b, mlstm_norm_g, w_out, ln1_g, ln1_b, w_gate, w_up, w_down, ln2_g, ln2_b):
    batch, seq, _ = x.shape
    assert seq % SPAN == 0
    x2 = x.reshape(batch * seq, D_MODEL)
    tab_a = _rope_table(seq, DIFF_QK, 128 // DIFF_QK)
    tab_c = _rope_table(seq, HEAD_DIM, 128 // HEAD_DIM)
    w_in_p = jnp.pad(w_in, ((0, 0), (0, 0), (0, IN_PAD - IN_WIDTH))).astype(BF16)
    w_out_b = w_out.astype(BF16)
    w_gate_b, w_up_b, w_down_b = w_gate.astype(BF16), w_up.astype(BF16), w_down.astype(BF16)
    for l in range(DEPTH):
        lam_init = 0.8 - 0.6 * math.exp(-0.3 * l)
        z, aq, ak, av, cq, ck = _inproj(x2, w_in_p, tab_a, tab_c, seq, l)
        out_a = _diff_attn(aq, ak, av, diff_lambda[l], diff_subln_g[l], lam_init, batch, seq)
        out_b = _sgu(z, sgu_ln_g[l], sgu_ln_b[l], sgu_w[l], sgu_b[l])
        out_c = _dilated(cq, ck, z, batch, seq)
        out_d = _mlstm(z, mlstm_conv_w[l], mlstm_conv_b[l], mlstm_gate_b[l], mlstm_norm_g[l], batch, seq)
        x2 = _mix_ffn(out_a, out_b, out_c, out_d, w_out_b, x2, ln1_g[l], ln1_b[l],
                      w_gate_b, w_up_b, w_down_b, ln2_g[l], ln2_b[l], l)
    return x2.reshape(batch, seq, D_MODEL)
```

```python
import functools
import math

import jax
import jax.numpy as jnp
from jax import lax
from jax.experimental import pallas as pl
from jax.experimental.pallas import tpu as pltpu

F32 = jnp.float32
BF16 = jnp.bfloat16

D_MODEL = 1024
DEPTH = 4
N_HEADS = 4
HEAD_DIM = 64
GROUP = N_HEADS * HEAD_DIM
DIFF_QK = HEAD_DIM // 2
CHUNK = 128
DILATIONS = (1, 4, 16)
SPAN = CHUNK * DILATIONS[-1]
DIL_GROUP = 4
CONV_WIDTH = 4
ROPE_THETA = 500000.0
D_FF = 2816
FF_CHUNK = 256
IN_WIDTH = 3080
IN_PAD = 3200
COL_AQ, COL_AK, COL_AV, COL_CQ, COL_CK = 0, 256, 512, 1280, 1536
REST_CHUNKS = ((768, 0, 512), (1792, 512, 512), (2304, 1024, 512), (2816, 1536, 384))
REST_WIDTH = 1920
REST_BU, REST_BV, REST_CV, REST_DQ, REST_DK, REST_DV, REST_DO = range(7)
REST_GATES = 1792 // 128
ATT_TILE = 256
ATT_TQ = 2 * ATT_TILE
ACC_ROWS = HEAD_DIM + 16
LOG2E = math.log2(math.e)
ALPHA = (2 * DEPTH) ** 0.25
LN_EPS = 1e-5
NEG_INF = float("-inf")
VMEM_LIMIT = 56 * 1024 * 1024


def _cparams(sem):
    return pltpu.CompilerParams(dimension_semantics=sem, vmem_limit_bytes=VMEM_LIMIT)


def _head_select(head_of_lane, parts):
    out = parts[-1]
    for h in range(len(parts) - 2, -1, -1):
        out = jnp.where(head_of_lane == h, parts[h], out)
    return out


def _layer_norm_rows(y, g, b):
    mu = jnp.mean(y, axis=-1, keepdims=True)
    yc = y - mu
    var = jnp.mean(yc * yc, axis=-1, keepdims=True)
    return yc * lax.rsqrt(var + LN_EPS) * g + b


def _rope(x, c, s_lo, s_hi, half):
    width = x.shape[1]
    return x * c + pltpu.roll(x, width - half, 1) * s_lo + pltpu.roll(x, half, 1) * s_hi


def _inproj_kernel(x_ref, w_ref, ta_ref, tc_ref, rest_ref, oaq_ref, oak_ref, oav_ref, ocq_ref, ock_ref):
    xb = x_ref[...].astype(BF16)

    def project(col0, width=GROUP):
        return jnp.dot(xb, w_ref[:, col0:col0 + width], preferred_element_type=F32)

    ta_c, ta_lo, ta_hi = [jnp.concatenate([ta_ref[t]] * (GROUP // 128), axis=1) for t in range(3)]
    tc_c, tc_lo, tc_hi = [jnp.concatenate([tc_ref[t]] * (GROUP // 128), axis=1) for t in range(3)]
    half_a = DIFF_QK // 8
    half_c = HEAD_DIM // 8
    q = _rope(project(COL_AQ), ta_c, ta_lo, ta_hi, half_a) * (DIFF_QK ** -0.5 * LOG2E)
    oak_ref[...] = _rope(project(COL_AK), ta_c, ta_lo, ta_hi, half_a).astype(BF16)
    v = project(COL_AV)
    for t in range(q.shape[0] // ATT_TILE):
        rows = slice(t * ATT_TILE, (t + 1) * ATT_TILE)
        oaq_ref[t] = jnp.transpose(q[rows])
        oav_ref[t] = jnp.transpose(v[rows]).astype(BF16)
    ocq_ref[...] = _rope(project(COL_CQ), tc_c, tc_lo, tc_hi, half_c) * (HEAD_DIM ** -0.5 * LOG2E)
    ock_ref[...] = _rope(project(COL_CK), tc_c, tc_lo, tc_hi, half_c)
    for src, dst, width in REST_CHUNKS:
        rest_ref[:, dst:dst + width] = project(src, width)


def _inproj(x2, w, tab_a, tab_c, seq, layer):
    m = x2.shape[0]
    tm = 512
    nt = seq // tm
    tspec = pl.BlockSpec((3, tm, 128), lambda i: (0, i % nt, 0))
    ospec = pl.BlockSpec((tm, GROUP), lambda i: (i, 0))
    tspec_out = pl.BlockSpec((tm // ATT_TILE, GROUP, ATT_TILE), lambda i: (i, 0, 0))
    transposed = lambda dt: jax.ShapeDtypeStruct((m // ATT_TILE, GROUP, ATT_TILE), dt)
    return pl.pallas_call(
        _inproj_kernel,
        grid=(m // tm,),
        in_specs=[pl.BlockSpec((tm, D_MODEL), lambda i: (i, 0)),
                  _layer_weight(layer, (D_MODEL, IN_PAD)), tspec, tspec],
        out_specs=[pl.BlockSpec((tm, REST_WIDTH), lambda i: (i, 0)), tspec_out, ospec, tspec_out, ospec, ospec],
        out_shape=[jax.ShapeDtypeStruct((m, REST_WIDTH), F32), transposed(F32),
                   jax.ShapeDtypeStruct((m, GROUP), BF16), transposed(BF16),
                   jax.ShapeDtypeStruct((m, GROUP), F32), jax.ShapeDtypeStruct((m, GROUP), F32)],
        compiler_params=_cparams(("parallel",)),
        name="inproj",
    )(x2, w, tab_a, tab_c)


def _rope_table(seq, group_dim, n_groups):
    rot = group_dim // 4
    half = rot // 2
    pos = jnp.arange(seq, dtype=F32)
    inv = ROPE_THETA ** (-jnp.arange(0, rot, 2, dtype=F32) / rot)
    ang = pos[:, None] * inv[None, :]
    cos, sin = jnp.cos(ang), jnp.sin(ang)
    ones = jnp.ones((seq, group_dim - rot), F32)
    zeros_half = jnp.zeros((seq, half), F32)
    zeros_rest = jnp.zeros((seq, group_dim - rot), F32)
    c = jnp.concatenate([cos, cos, ones], axis=1)
    lo = jnp.concatenate([-sin, zeros_half, zeros_rest], axis=1)
    hi = jnp.concatenate([zeros_half, sin, zeros_rest], axis=1)
    return jnp.stack([jnp.tile(t, (1, n_groups)) for t in (c, lo, hi)], axis=0)


def _diff_attn_kernel(lamv_ref, g_ref, qt_ref, k_ref, vt_ref, o_ref, qm_ref, m_ref, acc_ref,
                      sa_ref, sb_ref, xa_ref, xb_ref, *, lam_init):
    tq, tk = ATT_TQ, ATT_TILE
    i = pl.program_id(1)
    map_of_row = lax.broadcasted_iota(jnp.int32, (GROUP, 1), 0) // DIFF_QK
    qt = jnp.concatenate([qt_ref[c] for c in range(tq // ATT_TILE)], axis=1)
    for hm in range(2 * N_HEADS):
        qm_ref[hm] = jnp.where(map_of_row == hm, qt, 0.0).astype(BF16)
    m_ref[...] = jnp.full(m_ref.shape, NEG_INF, F32)
    acc_ref[...] = jnp.zeros(acc_ref.shape, F32)
    ones = jnp.ones((ACC_ROWS - HEAD_DIM, tk), BF16)

    def key_tile(j):
        return k_ref[pl.ds(pl.multiple_of(j * tk, tk), tk), :]

    every = slice(0, tq)

    def produce(dst, j, hm, cols=every):
        scores = jnp.dot(key_tile(j), qm_ref[hm, :, cols], preferred_element_type=F32)
        dst[0][hm, :, cols] = scores
        dst[1][hm:hm + 1, cols] = jnp.max(scores, axis=0, keepdims=True)

    def consume(src, j, hm, cols=every, bias=None):
        s = src[0][hm, :, cols]
        if bias is None:
            m_tile = src[1][hm:hm + 1, cols]
        else:
            s = s + bias
            m_tile = jnp.max(s, axis=0, keepdims=True)
        m_prev = m_ref[hm:hm + 1, cols]
        m_new = jnp.maximum(m_prev, m_tile)
        alpha = jnp.exp2(m_prev - m_new)
        p = jnp.exp2(s - m_new).astype(BF16)
        h = hm // 2
        v_ext = jnp.concatenate([vt_ref[j, h * HEAD_DIM:(h + 1) * HEAD_DIM, :], ones], axis=0)
        acc_ref[hm, :, cols] = alpha * acc_ref[hm, :, cols] + jnp.dot(v_ext, p, preferred_element_type=F32)
        m_ref[hm:hm + 1, cols] = m_new

    def stage(src, j, dst):
        for hm in range(2 * N_HEADS):
            produce(dst, j + 1, hm)
            consume(src, j, hm)

    buf_a, buf_b = (sa_ref, xa_ref), (sb_ref, xb_ref)
    for hm in range(2 * N_HEADS):
        produce(buf_a, 0, hm)

    n_plain = i

    def pair(first):
        stage(buf_a, first, buf_b)
        stage(buf_b, first + 1, buf_a)

    def two_pairs(u, carry):
        pair(4 * u)
        pair(4 * u + 2)
        return carry

    def one_pair(u, carry):
        pair(2 * u)
        return carry

    lax.fori_loop(0, n_plain // 2, two_pairs, 0)
    lax.fori_loop(2 * (n_plain // 2), n_plain, one_pair, 0)
    last = 2 * n_plain
    low, high = slice(0, tk), slice(tk, tq)
    causal_bias = jnp.where(lax.broadcasted_iota(jnp.int32, (tk, tk), 0)
                            <= lax.broadcasted_iota(jnp.int32, (tk, tk), 1), 0.0, NEG_INF)
    for hm in range(2 * N_HEADS):
        produce(buf_b, last + 1, hm, high)
        consume(buf_a, last, hm, low, causal_bias)
        consume(buf_a, last, hm, high)
    for hm in range(2 * N_HEADS):
        consume(buf_b, last + 1, hm, high, causal_bias)

    lv = lamv_ref[...]
    lam = (jnp.exp(jnp.sum(lv[0:1] * lv[1:2], axis=1, keepdims=True))
           - jnp.exp(jnp.sum(lv[2:3] * lv[3:4], axis=1, keepdims=True)) + lam_init)
    outs = []
    for h in range(N_HEADS):
        a0, a1 = acc_ref[2 * h], acc_ref[2 * h + 1]
        a = a0[:HEAD_DIM] / a0[HEAD_DIM:HEAD_DIM + 1] - lam * (a1[:HEAD_DIM] / a1[HEAD_DIM:HEAD_DIM + 1])
        ms = jnp.mean(a * a, axis=0, keepdims=True)
        outs.append(a * lax.rsqrt(ms + LN_EPS))
    o_ref[...] = (jnp.transpose(jnp.concatenate(outs, axis=0)) * g_ref[...] * (1.0 - lam_init)).astype(BF16)


def _diff_attn(aq_t, ak, av_t, lam_vecs, subln_g, lam_init, batch, seq):
    tq, tk = ATT_TQ, ATT_TILE
    nq = seq // tq
    maps = 2 * N_HEADS
    return pl.pallas_call(
        functools.partial(_diff_attn_kernel, lam_init=lam_init),
        grid=(batch, nq),
        in_specs=[pl.BlockSpec((4, DIFF_QK), lambda b, i: (0, 0)),
                  pl.BlockSpec((1, GROUP), lambda b, i: (0, 0)),
                  pl.BlockSpec((tq // tk, GROUP, tk), lambda b, i: (b * nq + i, 0, 0)),
                  pl.BlockSpec((seq, GROUP), lambda b, i: (b, 0)),
                  pl.BlockSpec((seq // tk, GROUP, tk), lambda b, i: (b, 0, 0))],
        out_specs=pl.BlockSpec((tq, GROUP), lambda b, i: (b * nq + i, 0)),
        out_shape=jax.ShapeDtypeStruct((batch * seq, GROUP), BF16),
        scratch_shapes=[pltpu.VMEM((maps, GROUP, tq), BF16),
                        pltpu.VMEM((maps, tq), F32),
                        pltpu.VMEM((maps, ACC_ROWS, tq), F32),
                        pltpu.VMEM((maps, tk, tq), F32), pltpu.VMEM((maps, tk, tq), F32),
                        pltpu.VMEM((maps, tq), F32), pltpu.VMEM((maps, tq), F32)],
        compiler_params=_cparams(("parallel", "arbitrary")),
        name="diff_attn",
    )(lam_vecs, jnp.tile(subln_g, N_HEADS)[None, :], aq_t, ak, av_t)


def _sgu_kernel(u_ref, v_ref, g_ref, b_ref, w_ref, bt_ref, o_ref, *, tm):
    lane = lax.broadcasted_iota(jnp.int32, (1, GROUP), 1)
    head_of_lane = lane // HEAD_DIM
    u = jax.nn.gelu(u_ref[...])
    vn = _layer_norm_rows(jax.nn.gelu(v_ref[...]), g_ref[...], b_ref[...]).astype(BF16)
    r = lax.broadcasted_iota(jnp.int32, (CHUNK, CHUNK), 0)
    c = lax.broadcasted_iota(jnp.int32, (CHUNK, CHUNK), 1)
    w_all = jnp.concatenate([jnp.where(c <= r, w_ref[g], 0.0) for g in range(N_HEADS)], axis=0).astype(BF16)
    for ch in range(tm // CHUNK):
        rows = slice(ch * CHUNK, (ch + 1) * CHUNK)
        res = jnp.dot(w_all, vn[rows], preferred_element_type=F32)
        zc = _head_select(head_of_lane, [res[g * CHUNK:(g + 1) * CHUNK] for g in range(N_HEADS)])
        o_ref[rows, :] = (u[rows] * (zc + bt_ref[...])).astype(BF16)


def _sgu(z, ln_g, ln_b, w_s, b_s):
    m = z.shape[0]
    tm = 512
    bias_tab = jnp.repeat(jnp.transpose(b_s), HEAD_DIM, axis=1)
    return pl.pallas_call(
        functools.partial(_sgu_kernel, tm=tm),
        grid=(m // tm,),
        in_specs=[pl.BlockSpec((tm, GROUP), lambda i: (i, REST_BU)),
                  pl.BlockSpec((tm, GROUP), lambda i: (i, REST_BV)),
                  pl.BlockSpec((1, GROUP), lambda i: (0, 0)),
                  pl.BlockSpec((1, GROUP), lambda i: (0, 0)),
                  pl.BlockSpec((N_HEADS, CHUNK, CHUNK), lambda i: (0, 0, 0)),
                  pl.BlockSpec((CHUNK, GROUP), lambda i: (0, 0))],
        out_specs=pl.BlockSpec((tm, GROUP), lambda i: (i, 0)),
        out_shape=jax.ShapeDtypeStruct((m, GROUP), BF16),
        compiler_params=_cparams(("parallel",)),
        name="sgu",
    )(z, z, ln_g[None, :], ln_b[None, :], w_s, bias_tab)


def _dilated_kernel(*refs):
    q_ref, kc_ref, kp_ref, vc_ref, vp_ref = [refs[2 * t:2 * t + 2] for t in range(5)]
    o_ref = refs[10:12]
    os_ref = [refs[12 + 2 * p:14 + 2 * p] for p in range(3)]
    ls_ref = [refs[18 + 2 * p:20 + 2 * p] for p in range(3)]
    n = pl.program_id(1)
    lane = lax.broadcasted_iota(jnp.int32, (1, GROUP), 1)
    head_of_lane = lane // HEAD_DIM
    i_idx = lax.broadcasted_iota(jnp.int32, (CHUNK, 2 * CHUNK), 0)
    j_idx = lax.broadcasted_iota(jnp.int32, (CHUNK, 2 * CHUNK), 1)
    dist = CHUNK + i_idx - j_idx
    band_bias = jnp.where((dist >= 0) & (dist <= CHUNK), 0.0, NEG_INF)
    prev_half_bias = jnp.where(j_idx >= CHUNK, 0.0, NEG_INF)

    def rows(start, d):
        return pl.ds(start, CHUNK) if d == 1 else pl.ds(start, CHUNK, stride=d)

    def strided(pair, start, d):
        return jnp.concatenate([pair[0][rows(start, d), :], pair[1][rows(start, d), :]], axis=1)

    def store(pair, start, d, val):
        pair[0][rows(start, d), :] = val[:, :128]
        pair[1][rows(start, d), :] = val[:, 128:]

    def scores(d, r, nb, first):
        nblk = SPAN // (CHUNK * d)
        start = nb * (CHUNK * d) + r
        in_prev = (nblk - 1) * (CHUNK * d) + r
        if first is True:
            k_prev, v_prev = strided(kp_ref, in_prev, d), strided(vp_ref, in_prev, d)
        elif first is False:
            k_prev, v_prev = strided(kc_ref, start - CHUNK * d, d), strided(vc_ref, start - CHUNK * d, d)
        else:
            first = nb == 0
            in_span = jnp.maximum(start - CHUNK * d, 0)
            k_prev = jnp.where(first, strided(kp_ref, in_prev, d), strided(kc_ref, in_span, d))
            v_prev = jnp.where(first, strided(vp_ref, in_prev, d), strided(vc_ref, in_span, d))
        qb = strided(q_ref, start, d).astype(BF16)
        kk = jnp.concatenate([k_prev, strided(kc_ref, start, d)], axis=0).astype(BF16)
        vv = jnp.concatenate([v_prev, strided(vc_ref, start, d)], axis=0).astype(BF16)
        q_zero = jnp.zeros_like(qb)
        q_heads = jnp.concatenate([jnp.where(head_of_lane == h, qb, q_zero) for h in range(N_HEADS)], axis=0)
        s_all = lax.dot_general(q_heads, kk, (((1,), (1,)), ((), ())), preferred_element_type=F32)
        ss = [s_all[h * CHUNK:(h + 1) * CHUNK] for h in range(N_HEADS)]
        if first is False:
            bias = band_bias
        elif first is True:
            bias = band_bias + jnp.where(n == 0, prev_half_bias, 0.0)
        else:
            bias = band_bias + jnp.where(first & (n == 0), prev_half_bias, 0.0)
        return ss, vv, bias, start

    def finish(pi, d, ss, vv, bias, start):
        probs, lses = [], []
        for h in range(N_HEADS):
            s = ss[h] + bias
            mx = jnp.max(s, axis=1, keepdims=True)
            p = jnp.exp2(s - mx)
            den = jnp.sum(p, axis=1, keepdims=True)
            probs.append((p / den).astype(BF16))
            lses.append(mx + jnp.log2(den))
        v_zero = jnp.zeros_like(vv)
        v_bd = jnp.concatenate([jnp.where(head_of_lane == h, vv, v_zero) for h in range(N_HEADS)], axis=0)
        ob = jnp.dot(jnp.concatenate(probs, axis=1), v_bd, preferred_element_type=F32)
        lb = _head_select(head_of_lane, lses) + jnp.zeros((CHUNK, GROUP), F32)
        store(os_ref[pi], start, d, ob)
        store(ls_ref[pi], start, d, lb)

    blocks_per_pattern = SPAN // CHUNK
    for pi, d in enumerate(DILATIONS):
        def group(t, carry, pi=pi, d=d):
            trips = blocks_per_pattern // DIL_GROUP
            nblk = blocks_per_pattern // d
            if nblk == 1:
                staged = [scores(d, t + g * trips, 0, True) for g in range(DIL_GROUP)]
            elif nblk == DIL_GROUP:
                staged = [scores(d, t, g, g == 0) for g in range(DIL_GROUP)]
            else:
                assert d == 1
                staged = [scores(d, 0, t + g * trips, None if g == 0 else False) for g in range(DIL_GROUP)]
            for item in staged:
                finish(pi, d, *item)
            return carry

        lax.fori_loop(0, blocks_per_pattern // DIL_GROUP, group, 0)

    for half in range(2):
        l0, l1, l2 = ls_ref[0][half][...], ls_ref[1][half][...], ls_ref[2][half][...]
        mx = jnp.maximum(jnp.maximum(l0, l1), l2)
        w0, w1, w2 = jnp.exp2(l0 - mx), jnp.exp2(l1 - mx), jnp.exp2(l2 - mx)
        tot = w0 + w1 + w2
        o_ref[half][...] = ((w0 / tot) * os_ref[0][half][...] + (w1 / tot) * os_ref[1][half][...]
                            + (w2 / tot) * os_ref[2][half][...]).astype(BF16)


def _dilated(cq, ck, z, batch, seq):
    ns = seq // SPAN
    blk = (SPAN, 128)

    def halves(col, prev):
        def spec(c):
            if prev:
                return pl.BlockSpec(blk, lambda b, n: (b * ns + jnp.maximum(n - 1, 0), c))
            return pl.BlockSpec(blk, lambda b, n: (b * ns + n, c))
        return [spec(2 * col), spec(2 * col + 1)]

    v_col = REST_CV
    return pl.pallas_call(
        _dilated_kernel,
        grid=(batch, ns),
        in_specs=(halves(0, False) + halves(0, False) + halves(0, True)
                  + halves(v_col, False) + halves(v_col, True)),
        out_specs=[pl.BlockSpec(blk, lambda b, n: (b * ns + n, 0))] * 2,
        out_shape=[jax.ShapeDtypeStruct((batch * seq, 128), BF16)] * 2,
        scratch_shapes=[pltpu.VMEM(blk, F32)] * 12,
        compiler_params=_cparams(("parallel", "arbitrary")),
        name="dilated",
    )(cq, cq, ck, ck, ck, ck, z, z, z, z)


def _lane_pick(x, lane, idx):
    return jnp.sum(jnp.where(lane == idx, x, 0.0), axis=1, keepdims=True)


def _mlstm_kernel(*refs, n_seq):
    ins = [refs[5 * p:5 * p + 5] for p in range(n_seq)]
    cw_ref, cb_ref, gb_ref, ng_ref = refs[5 * n_seq:5 * n_seq + 4]
    o_ref = refs[5 * n_seq + 4]
    states = [refs[5 * n_seq + 5 + 4 * p:5 * n_seq + 9 + 4 * p] for p in range(n_seq)]
    c_idx = pl.program_id(0)
    L = CHUNK
    lane = lax.broadcasted_iota(jnp.int32, (1, GROUP), 1)
    head_of_lane = lane // HEAD_DIM
    lane1 = lax.broadcasted_iota(jnp.int32, (1, 128), 1)
    row_head = lax.broadcasted_iota(jnp.int32, (GROUP, 1), 0) // HEAD_DIM
    rows = lax.broadcasted_iota(jnp.int32, (L, 1), 0)
    tril = lax.broadcasted_iota(jnp.int32, (L, L), 1) <= lax.broadcasted_iota(jnp.int32, (L, L), 0)
    tril_ones = jnp.where(tril, 1.0, 0.0)

    @pl.when(c_idx == 0)
    def _():
        for state in states:
            for ref in state:
                ref[...] = jnp.zeros(ref.shape, F32)

    def sequence(p):
        q_ref, k_ref, v_ref, og_ref, gt_ref = ins[p]
        tail_ref, ct_ref, nm_ref, ms_ref = states[p]

        gates = gt_ref[...] + gb_ref[...]
        log_f = jnp.minimum(gates, 0.0) - jnp.log1p(jnp.exp(-jnp.abs(gates)))
        xg = jnp.where(lane1 < N_HEADS, gates, jnp.where(lane1 < 2 * N_HEADS, log_f, 0.0))
        csum = jnp.dot(tril_ones, xg, preferred_element_type=F32,
                       precision=lax.Precision.HIGHEST)

        x = jnp.concatenate([q_ref[...], k_ref[...]], axis=1)
        tail = tail_ref[...]
        y = x * cw_ref[CONV_WIDTH - 1:CONV_WIDTH, :]
        for back in range(1, CONV_WIDTH):
            rolled = pltpu.roll(x, back, 0)
            head = jnp.where(rows[:8] >= back, rolled[:8], pltpu.roll(tail, back, 0))
            shifted = jnp.concatenate([head, rolled[8:]], axis=0)
            y = y + shifted * cw_ref[CONV_WIDTH - 1 - back:CONV_WIDTH - back, :]
        tail_ref[...] = x[L - 8:, :]
        qk = jax.nn.silu(y + cb_ref[...])
        mq = qk[:, :GROUP].astype(BF16)
        mk = (qk[:, GROUP:] * (HEAD_DIM ** -0.5)).astype(BF16)
        v = v_ref[...]
        vb = v.astype(BF16)
        yield

        q_zero = jnp.zeros_like(mq)
        q_heads = jnp.concatenate([jnp.where(head_of_lane == h, mq, q_zero) for h in range(N_HEADS)], axis=0)
        qk_all = lax.dot_general(q_heads, mk, (((1,), (1,)), ((), ())), preferred_element_type=F32)
        inter_num = jnp.dot(mq, ct_ref[...].astype(BF16), preferred_element_type=F32)
        nq = jnp.dot(mq, nm_ref[...].astype(BF16), preferred_element_type=F32)
        yield

        xg_t = jnp.transpose(xg)
        csum_t = jnp.transpose(csum)
        m_state = ms_ref[...]
        sws, e_cols, mout_cols, den_cols, w_cols = [], [], [], [], []
        a_sc, e_sc, m_news = [], [], []
        for h in range(N_HEADS):
            b_c = _lane_pick(csum, lane1, N_HEADS + h)
            i_c = _lane_pick(xg, lane1, h)
            b_r = csum_t[N_HEADS + h:N_HEADS + h + 1, :]
            i_r = xg_t[h:h + 1, :]
            b_last = _lane_pick(b_r, lane1, L - 1)
            m_prev = _lane_pick(m_state, lane1, h)
            d_log = jnp.where(tril, b_c - b_r + i_r, NEG_INF)
            g_c = b_last - b_c + i_c
            g_max = jnp.max(g_c, axis=0, keepdims=True)
            w_cols.append(jnp.exp(g_c - g_max))
            inter = b_c + m_prev
            m_out = jnp.maximum(inter, jnp.max(d_log, axis=1, keepdims=True))
            p_mat = jnp.exp(d_log - m_out)
            sw = p_mat * qk_all[h * L:(h + 1) * L]
            e_t = jnp.exp(inter - m_out)
            den_cols.append(jnp.sum(sw, axis=1, keepdims=True) + e_t * _lane_pick(nq, lane1, h))
            sws.append(sw.astype(BF16))
            e_cols.append(e_t)
            mout_cols.append(m_out)
            m_new = jnp.maximum(b_last + m_prev, g_max)
            a_sc.append(jnp.exp(b_last + m_prev - m_new))
            e_sc.append(jnp.exp(g_max - m_new))
            m_news.append(m_new)
        yield

        v_zero = jnp.zeros_like(vb)
        v_bd = jnp.concatenate([jnp.where(head_of_lane == h, vb, v_zero) for h in range(N_HEADS)], axis=0)
        num = (jnp.dot(jnp.concatenate(sws, axis=1), v_bd, preferred_element_type=F32)
               + _head_select(head_of_lane, e_cols) * inter_num)
        wv = (_head_select(head_of_lane, w_cols) * v).astype(BF16)
        ct_loc = lax.dot_general(mk, wv, (((0,), (0,)), ((), ())), preferred_element_type=F32)
        w_mat = jnp.zeros((L, 128), F32)
        for h in range(N_HEADS):
            w_mat = jnp.where(lane1 == h, w_cols[h], w_mat)
        nm_loc = lax.dot_general(mk, w_mat.astype(BF16), (((0,), (0,)), ((), ())), preferred_element_type=F32)
        yield

        floors = [jnp.maximum(jnp.abs(den_cols[h]), jnp.exp(-mout_cols[h])) for h in range(N_HEADS)]
        hid = num / _head_select(head_of_lane, floors)
        mus = [jnp.sum(jnp.where(head_of_lane == h, hid, 0.0), axis=1, keepdims=True) * (1.0 / HEAD_DIM)
               for h in range(N_HEADS)]
        hc = hid - _head_select(head_of_lane, mus)
        hc2 = hc * hc
        var = [jnp.sum(jnp.where(head_of_lane == h, hc2, 0.0), axis=1, keepdims=True) * (1.0 / HEAD_DIM)
               for h in range(N_HEADS)]
        normed = hc * lax.rsqrt(_head_select(head_of_lane, var) + LN_EPS) * ng_ref[...]
        o_ref[p] = (jax.nn.sigmoid(og_ref[...]) * normed).astype(BF16)

        ct_loc = jnp.where(row_head == head_of_lane, ct_loc, 0.0)
        ct_ref[...] = _head_select(row_head, a_sc) * ct_ref[...] + _head_select(row_head, e_sc) * ct_loc
        nm_loc = jnp.where(row_head == lane1, nm_loc, 0.0)
        nm_ref[...] = _head_select(row_head, a_sc) * nm_ref[...] + _head_select(row_head, e_sc) * nm_loc
        m_row = jnp.zeros((1, 128), F32)
        for h in range(N_HEADS):
            m_row = jnp.where(lane1 == h, m_news[h], m_row)
        ms_ref[...] = m_row
        yield

    chains = [sequence(p) for p in range(n_seq)]
    for _ in range(5):
        for chain in chains:
            next(chain)


def _mlstm(z, conv_w, conv_b, gate_b, norm_g, batch, seq):
    nc = seq // CHUNK
    const = lambda c: (0, 0)
    gate_tab = jnp.zeros((1, 128), F32).at[0, :2 * N_HEADS].set(gate_b.reshape(-1))
    seq_specs = []
    for b in range(batch):
        row = lambda col, b=b: (lambda c: (b * nc + c, col))
        seq_specs += [pl.BlockSpec((CHUNK, GROUP), row(REST_DQ)), pl.BlockSpec((CHUNK, GROUP), row(REST_DK)),
                      pl.BlockSpec((CHUNK, GROUP), row(REST_DV)), pl.BlockSpec((CHUNK, GROUP), row(REST_DO)),
                      pl.BlockSpec((CHUNK, 128), row(REST_GATES))]
    state = [pltpu.VMEM((8, 2 * GROUP), F32), pltpu.VMEM((GROUP, GROUP), F32),
             pltpu.VMEM((GROUP, 128), F32), pltpu.VMEM((1, 128), F32)]
    return pl.pallas_call(
        functools.partial(_mlstm_kernel, n_seq=batch),
        grid=(nc,),
        in_specs=seq_specs + [pl.BlockSpec((CONV_WIDTH, 2 * GROUP), const), pl.BlockSpec((1, 2 * GROUP), const),
                              pl.BlockSpec((1, 128), const), pl.BlockSpec((1, GROUP), const)],
        out_specs=pl.BlockSpec((batch, CHUNK, GROUP), lambda c: (0, c, 0)),
        out_shape=jax.ShapeDtypeStruct((batch, seq, GROUP), BF16),
        scratch_shapes=state * batch,
        compiler_params=_cparams(("arbitrary",)),
        name="mlstm",
    )(*([z] * (5 * batch)), conv_w, conv_b[None, :], gate_tab,
      jnp.tile(norm_g, N_HEADS)[None, :]).reshape(batch * seq, GROUP)


def _mix_ffn_kernel(a_ref, b_ref, c_lo_ref, c_hi_ref, d_ref, wo_ref, x_ref, g1_ref, b1_ref,
                    wg_ref, wu_ref, wd_ref, g2_ref, b2_ref, o_ref):
    mixed = jnp.concatenate([a_ref[...], b_ref[...], c_lo_ref[...], c_hi_ref[...], d_ref[...]], axis=1)
    h = jnp.dot(mixed, wo_ref[...], preferred_element_type=F32)
    x1 = _layer_norm_rows(ALPHA * x_ref[...] + h, g1_ref[...], b1_ref[...])
    xb = x1.astype(BF16)
    hidden = []
    for c in range(D_FF // FF_CHUNK):
        cols = slice(c * FF_CHUNK, (c + 1) * FF_CHUNK)
        gate = jnp.dot(xb, wg_ref[:, cols], preferred_element_type=F32)
        up = jnp.dot(xb, wu_ref[:, cols], preferred_element_type=F32)
        hidden.append((jax.nn.silu(gate) * up).astype(BF16))
    f = jnp.dot(jnp.concatenate(hidden, axis=1), wd_ref[...], preferred_element_type=F32)
    o_ref[...] = _layer_norm_rows(ALPHA * x1 + f, g2_ref[...], b2_ref[...])


def _layer_weight(layer, shape):
    return pl.BlockSpec((None,) + shape, lambda i: (layer, 0, 0), pipeline_mode=pl.Buffered(1))


def _mix_ffn(oa, ob, oc, od, wo, x2, g1, b1, wg, wu, wd, g2, b2, layer):
    m = x2.shape[0]
    tm = 512
    part = pl.BlockSpec((tm, GROUP), lambda i: (i, 0))
    half = pl.BlockSpec((tm, 128), lambda i: (i, 0))
    full = pl.BlockSpec((tm, D_MODEL), lambda i: (i, 0))
    vec = pl.BlockSpec((1, D_MODEL), lambda i: (0, 0))
    whole = functools.partial(_layer_weight, layer)
    return pl.pallas_call(
        _mix_ffn_kernel,
        grid=(m // tm,),
        in_specs=[part, part, half, half, part, whole((D_MODEL, D_MODEL)), full, vec, vec,
                  whole((D_MODEL, D_FF)), whole((D_MODEL, D_FF)), whole((D_FF, D_MODEL)), vec, vec],
        out_specs=full,
        out_shape=jax.ShapeDtypeStruct((m, D_MODEL), F32),
        compiler_params=_cparams(("parallel",)),
        name="mix_ffn",
    )(oa, ob, oc[0], oc[1], od, wo, x2, g1[None, :], b1[None, :], wg, wu, wd, g2[None, :], b2[None, :])


def kernel(x, w_in, diff_lambda, diff_subln_g, sgu_ln_g, sgu_ln_b, sgu_w, sgu_b, mlstm_conv_w, mlstm_conv_b,
           mlstm_gate_b, mlstm_norm_g, w_out, ln1_g, ln1_b, w_gate, w_up, w_down, ln2_g, ln2_b):
    batch, seq, _ = x.shape
    assert seq % SPAN == 0
    x2 = x.reshape(batch * seq, D_MODEL)
    tab_a = _rope_table(seq, DIFF_QK, 128 // DIFF_QK)
    tab_c = _rope_table(seq, HEAD_DIM, 128 // HEAD_DIM)
    w_in_p = jnp.pad(w_in, ((0, 0), (0, 0), (0, IN_PAD - IN_WIDTH))).astype(BF16)
    w_out_b = w_out.astype(BF16)
    w_gate_b, w_up_b, w_down_b = w_gate.astype(BF16), w_up.astype(BF16), w_down.astype(BF16)
    for l in range(DEPTH):
        lam_init = 0.8 - 0.6 * math.exp(-0.3 * l)
        z, aq, ak, av, cq, ck = _inproj(x2, w_in_p, tab_a, tab_c, seq, l)
        out_a = _diff_attn(aq, ak, av, diff_lambda[l], diff_subln_g[l], lam_init, batch, seq)
        out_b = _sgu(z, sgu_ln_g[l], sgu_ln_b[l], sgu_w[l], sgu_b[l])
        out_c = _dilated(cq, ck, z, batch, seq)
        out_d = _mlstm(z, mlstm_conv_w[l], mlstm_conv_b[l], mlstm_gate_b[l], mlstm_norm_g[l], batch, seq)
        x2 = _mix_ffn(out_a, out_b, out_c, out_d, w_out_b, x2, ln1_g[l], ln1_b[l],
                      w_gate_b, w_up_b, w_down_b, ln2_g[l], ln2_b[l], l)
    return x2.reshape(batch, seq, D_MODEL)
```
